```python
import math
import jax, jax.numpy as jnp
from jax import lax
import numpy as np

D_MODEL = 2048
BATCH = 4
SEQ = 2048
DEPTH = 4

GRID_W = 64
CTX_LEN = 256
HEAD_DIM = 128
NA_HEADS = 6
CV_GROUPS = 4
GD_HEADS = 6
NA_D = NA_HEADS * HEAD_DIM
CV_D = CV_GROUPS * HEAD_DIM
GD_D = GD_HEADS * HEAD_DIM
D_MIX = NA_D + CV_D + GD_D
WIN_R = 8
WIN_C = 16
CV_KSIZE = 3
GD_KSIZE = 5
GD_CHUNK = 64
ROPE_BASE = 10000.0
D_FF = 5632
N_ADA = 9
SPLIT_SIZES = (3 * NA_D, 3 * CV_D, 3 * GD_D, GD_D, 2 * GD_HEADS, 2 * GD_HEADS)
N_IN = sum(SPLIT_SIZES)
ALPHA = (2 * DEPTH) ** 0.25
BETA_INIT = (8 * DEPTH) ** -0.25
LN_EPS = 1e-6
NEG_INF = -1e30

kernel_name = 'hybrid_natten_shortconv_gdn_dit'


def _ln_plain(x):
    xf = x.astype(jnp.float32)
    mu = xf.mean(-1, keepdims=True)
    var = jnp.square(xf - mu).mean(-1, keepdims=True)
    return ((xf - mu) * lax.rsqrt(var + LN_EPS)).astype(x.dtype)


def _post_norm(x, y, g, b):
    return _ln_plain(ALPHA * x + y) * g + b


def _modulate(x, shift, scale):
    return _ln_plain(x) * (1.0 + scale) + shift


def _rms(x):
    xf = x.astype(jnp.float32)
    return (xf * lax.rsqrt(jnp.mean(xf * xf, -1, keepdims=True) + LN_EPS)).astype(x.dtype)


def _l2norm(x):
    xf = x.astype(jnp.float32)
    return (xf * lax.rsqrt(jnp.sum(xf * xf, -1, keepdims=True) + LN_EPS)).astype(x.dtype)


def _swiglu(h, w_gu, w_down):
    gate, up = jnp.split(h @ w_gu, 2, axis=-1)
    return (jax.nn.silu(gate) * up) @ w_down


def _dwconv(x, w):
    k = w.shape[0]
    return lax.conv_general_dilated(x, w[:, None, :], window_strides=(1,), padding=[(k // 2, k // 2)],
                                    dimension_numbers=('NWC', 'WIO', 'NWC'), feature_group_count=x.shape[-1])


def _heads(a, n):
    b, t, _ = a.shape
    return a.reshape(b, t, n, HEAD_DIM).transpose(0, 2, 1, 3)


def _merge(a):
    b, h, t, d = a.shape
    return a.transpose(0, 2, 1, 3).reshape(b, t, h * d)


def _axial_angles(n_tok):
    t = jnp.arange(n_tok, dtype=jnp.int32)
    row = (t // GRID_W).astype(jnp.float32)
    col = (t % GRID_W).astype(jnp.float32)
    n_freq = HEAD_DIM // 4
    inv_freq = ROPE_BASE ** (-jnp.arange(n_freq, dtype=jnp.float32) / n_freq)
    return row[:, None] * inv_freq, col[:, None] * inv_freq


def _rotate(x, ang):
    x1, x2 = jnp.split(x, 2, axis=-1)
    cos, sin = jnp.cos(ang).astype(x.dtype), jnp.sin(ang).astype(x.dtype)
    return jnp.concatenate([x1 * cos - x2 * sin, x1 * sin + x2 * cos], axis=-1)


def _axial_rope(x, ang_r, ang_c):
    xr, xc = jnp.split(x, 2, axis=-1)
    return jnp.concatenate([_rotate(xr, ang_r), _rotate(xc, ang_c)], axis=-1)


def _dense_attention(q, k, v):
    s = jnp.einsum('bhqd,bhkd->bhqk', q, k).astype(jnp.float32)
    p = jax.nn.softmax(s, axis=-1).astype(v.dtype)
    return jnp.einsum('bhqk,bhkd->bhqd', p, v)


def _neighbourhood_attention(q, k, v, k_ctx, v_ctx, rpb):
    b, h, t, dh = q.shape
    rows = t // GRID_W
    wr = min(WIN_R, rows)
    r = jnp.arange(rows)
    row_idx = jnp.clip(r - wr // 2, 0, rows - wr)[:, None] + jnp.arange(wr)[None, :]
    cc = jnp.arange(GRID_W)
    col_start = jnp.clip(cc - WIN_C // 2, 0, GRID_W - WIN_C)
    col_ok = (cc[None, :] >= col_start[:, None]) & (cc[None, :] < col_start[:, None] + WIN_C)
    dr = row_idx - r[:, None] + WIN_R - 1
    dc = jnp.clip(cc[None, :] - cc[:, None] + WIN_C - 1, 0, 2 * WIN_C - 2)
    bias = rpb[:, dr[:, None, :, None], dc[None, :, None, :]].astype(jnp.float32)
    qg = q.reshape(b, h, rows, GRID_W, dh)
    kb = k.reshape(b, h, rows, GRID_W, dh)[:, :, row_idx]
    vb = v.reshape(b, h, rows, GRID_W, dh)[:, :, row_idx]
    s_loc = jnp.einsum('bhrqd,bhrwkd->bhrqwk', qg, kb).astype(jnp.float32) + bias
    s_loc = jnp.where(col_ok[:, None, :], s_loc, NEG_INF).reshape(b, h, rows, GRID_W, wr * GRID_W)
    s_ctx = jnp.einsum('bhrqd,bhld->bhrql', qg, k_ctx).astype(jnp.float32)
    p = jax.nn.softmax(jnp.concatenate([s_loc, s_ctx], axis=-1), axis=-1).astype(v.dtype)
    n_loc = wr * GRID_W
    o = (jnp.einsum('bhrqn,bhrnd->bhrqd', p[..., :n_loc], vb.reshape(b, h, rows, n_loc, dh))
         + jnp.einsum('bhrql,bhld->bhrqd', p[..., n_loc:], v_ctx))
    return o.reshape(b, h, t, dh)


def _gated_delta_chunked(q, k, v, g, beta, state):
    out_dtype = v.dtype
    q, k, v, g, beta, state = [a.astype(jnp.float32) for a in (q, k, v, g, beta, state)]
    b, h, t, dk = q.shape
    dv = v.shape[-1]
    c = GD_CHUNK
    n = t // c
    q, k, v = [a.reshape(b, h, n, c, a.shape[-1]) for a in (q, k, v)]
    g, beta = g.reshape(b, h, n, c), beta.reshape(b, h, n, c)
    gc = jnp.cumsum(g, axis=-1)
    incl = jnp.tril(jnp.ones((c, c), bool))
    strict = jnp.tril(jnp.ones((c, c), bool), -1)
    decay = jnp.where(incl, jnp.exp(jnp.where(incl, gc[..., :, None] - gc[..., None, :], 0.0)), 0.0)
    kb = k * beta[..., None]
    a_mat = jnp.where(strict, jnp.einsum('bhncd,bhnjd->bhncj', kb, k) * decay, 0.0) + jnp.eye(c, dtype=jnp.float32)
    rhs = jnp.concatenate([v * beta[..., None], kb * jnp.exp(gc)[..., None]], axis=-1)
    sol = lax.linalg.triangular_solve(a_mat, rhs, left_side=True, lower=True, unit_diagonal=True)
    u, w = sol[..., :dv], sol[..., dv:]
    attn = jnp.where(incl, jnp.einsum('bhncd,bhnjd->bhncj', q, k) * decay, 0.0)
    q_dec = q * jnp.exp(gc)[..., None]
    k_tail = k * jnp.exp(gc[..., -1:] - gc)[..., None]
    g_tot = jnp.exp(gc[..., -1])

    def step(s, xs):
        q_i, k_i, u_i, w_i, a_i, gt_i = xs
        v_new = u_i - jnp.einsum('bhcd,bhde->bhce', w_i, s)
        o = jnp.einsum('bhcd,bhde->bhce', q_i, s) + jnp.einsum('bhcj,bhje->bhce', a_i, v_new)
        s = s * gt_i[..., None, None] + jnp.einsum('bhcd,bhce->bhde', k_i, v_new)
        return s, o

    xs = [jnp.moveaxis(a, 2, 0) for a in (q_dec, k_tail, u, w, attn, g_tot)]
    state, o = lax.scan(step, state, xs)
    o = jnp.moveaxis(o, 0, 2).reshape(b, h, t, dv)
    return o.astype(out_dtype), state


def _gdn_inputs(qkv, a_lin, b_lin, conv_w, a_log, dt_bias, ang):
    qkv = jax.nn.silu(_dwconv(qkv, conv_w))
    q, k, v = [_heads(a, GD_HEADS) for a in jnp.split(qkv, 3, axis=-1)]
    q, k = _l2norm(q), _l2norm(k)
    if ang is not None:
        q, k = _axial_rope(q, *ang), _axial_rope(k, *ang)
    q = q * HEAD_DIM ** -0.5
    b, t, _ = a_lin.shape
    a_lin = a_lin.reshape(b, t, 2, GD_HEADS).transpose(2, 0, 3, 1)
    b_lin = b_lin.reshape(b, t, 2, GD_HEADS).transpose(2, 0, 3, 1)
    g = -jnp.exp(a_log.astype(jnp.float32))[:, None, :, None] * jax.nn.softplus(
        (a_lin + dt_bias[:, None, :, None]).astype(jnp.float32))
    beta = jax.nn.sigmoid(b_lin.astype(jnp.float32))
    return q, k, v, g, beta


def _bidir_gdn(q, k, v, g, beta, s_f, s_b):
    o_f, s_f = _gated_delta_chunked(q, k, v, g[0], beta[0], s_f)
    fl = lambda a: jnp.flip(a, axis=2)
    o_b, s_b = _gated_delta_chunked(fl(q), fl(k), fl(v), fl(g[1]), fl(beta[1]), s_b)
    return o_f + fl(o_b), s_f, s_b


def _gdn_out(o, z, norm_w):
    b, h, t, d = o.shape
    o = o.transpose(0, 2, 1, 3)
    o = _rms(o) * norm_w * jax.nn.silu(z.reshape(b, t, h, d))
    return o.reshape(b, t, h * d)


def _mixer(hx, hc, w_in, rpb, cv_w, gd_conv_w, gd_a_log, gd_dt_bias, gd_norm_w, w_out, ang, ctx_out):
    split_at = np.cumsum(SPLIT_SIZES)[:-1].tolist()
    na_x, cv_x, qkv_x, z_x, beta_x, dec_x = jnp.split(hx @ w_in, split_at, axis=-1)
    na_c, cv_c, qkv_c, z_c, beta_c, dec_c = jnp.split(hc @ w_in, split_at, axis=-1)
    scale = HEAD_DIM ** -0.5
    qx, kx, vx = [_heads(a, NA_HEADS) for a in jnp.split(na_x, 3, axis=-1)]
    qc, kc, vc = [_heads(a, NA_HEADS) for a in jnp.split(na_c, 3, axis=-1)]
    o_na_x = _merge(_neighbourhood_attention(qx * scale, kx, vx, kc, vc, rpb))
    bx, cx, ux = jnp.split(cv_x, 3, axis=-1)
    o_cv_x = bx * _dwconv(cx * ux, cv_w)
    gq_c, gk_c, gv_c, g_c, be_c = _gdn_inputs(qkv_c, dec_c, beta_c, gd_conv_w, gd_a_log, gd_dt_bias, None)
    zeros = jnp.zeros(gq_c.shape[:2] + (HEAD_DIM, HEAD_DIM), jnp.float32)
    o_gd_c, s_f, s_b = _bidir_gdn(gq_c, gk_c, gv_c, g_c, be_c, zeros, zeros)
    gq_x, gk_x, gv_x, g_x, be_x = _gdn_inputs(qkv_x, dec_x, beta_x, gd_conv_w, gd_a_log, gd_dt_bias, ang)
    o_gd_x, _, _ = _bidir_gdn(gq_x, gk_x, gv_x, g_x, be_x, s_f, s_b)
    yx = jnp.concatenate([o_na_x, o_cv_x, _gdn_out(o_gd_x, z_x, gd_norm_w)], axis=-1) @ w_out
    if not ctx_out:
        return yx, None
    o_na_c = _merge(_dense_attention(qc * scale, kc, vc))
    bc, ccg, uc = jnp.split(cv_c, 3, axis=-1)
    o_cv_c = bc * _dwconv(ccg * uc, cv_w)
    yc = jnp.concatenate([o_na_c, o_cv_c, _gdn_out(o_gd_c, z_c, gd_norm_w)], axis=-1) @ w_out
    return yx, yc


def setup_inputs(seed: int = 0) -> dict:
    key = jax.random.key(seed)
    ks = jax.random.split(key, 20)
    f32 = jnp.float32
    d = D_MODEL
    nrm = lambda k, shape, s: s * jax.random.normal(k, shape, f32)
    dt = jnp.exp(jax.random.uniform(ks[16], (DEPTH, 2, GD_HEADS), f32, math.log(1e-3), math.log(1e-1)))
    return {
        'x': nrm(ks[0], (BATCH, SEQ, d), 1.0),
        'c': nrm(ks[1], (BATCH, d), 1.0),
        'ctx': nrm(ks[2], (BATCH, CTX_LEN, d), 1.0),
        'c_ctx': nrm(ks[3], (d,), 1.0),
        'w_ada': nrm(ks[4], (DEPTH, d, N_ADA * d), 0.5 * d ** -0.5),
        'b_ada': nrm(ks[5], (DEPTH, N_ADA * d), 0.01),
        'ln_g': 1.0 + nrm(ks[6], (DEPTH, 3, d), 0.02),
        'ln_b': nrm(ks[7], (DEPTH, 3, d), 0.02),
        'ffn1_w_gu': nrm(ks[8], (DEPTH, d, 2 * D_FF), d ** -0.5),
        'ffn1_w_down': nrm(ks[9], (DEPTH, D_FF, d), BETA_INIT * D_FF ** -0.5),
        'w_in': nrm(ks[10], (DEPTH, d, N_IN), d ** -0.5),
        'na_rpb': nrm(ks[11], (DEPTH, NA_HEADS, 2 * WIN_R - 1, 2 * WIN_C - 1), 0.1),
        'cv_conv_w': nrm(ks[12], (DEPTH, CV_KSIZE, CV_D), CV_KSIZE ** -0.5),
        'gd_conv_w': nrm(ks[13], (DEPTH, GD_KSIZE, 3 * GD_D), GD_KSIZE ** -0.5),
        'gd_a_log': jnp.log(jax.random.uniform(ks[14], (DEPTH, 2, GD_HEADS), f32, 1.0, 16.0)),
        'gd_dt_bias': dt + jnp.log(-jnp.expm1(-dt)),
        'gd_norm_w': 1.0 + nrm(ks[15], (DEPTH, HEAD_DIM), 0.02),
        'w_out': nrm(ks[17], (DEPTH, D_MIX, d), BETA_INIT * D_MIX ** -0.5),
        'ffn2_w_gu': nrm(ks[18], (DEPTH, d, 2 * D_FF), d ** -0.5),
        'ffn2_w_down': nrm(ks[19], (DEPTH, D_FF, d), BETA_INIT * D_FF ** -0.5),
    }


def reference(x, c, ctx, c_ctx, w_ada, b_ada, ln_g, ln_b, ffn1_w_gu, ffn1_w_down, w_in, na_rpb, cv_conv_w,
              gd_conv_w, gd_a_log, gd_dt_bias, gd_norm_w, w_out, ffn2_w_gu, ffn2_w_down):
    ang = _axial_angles(x.shape[1])
    silu_c = jax.nn.silu(c)
    silu_cc = jax.nn.silu(c_ctx)
    for l in range(DEPTH):
        last = l == DEPTH - 1
        ada_x = [a[:, None, :] for a in jnp.split(silu_c @ w_ada[l] + b_ada[l], N_ADA, axis=-1)]
        ada_c = jnp.split(silu_cc @ w_ada[l] + b_ada[l], N_ADA, axis=-1)
        hx = _modulate(x, ada_x[0], ada_x[1])
        hc = _modulate(ctx, ada_c[0], ada_c[1])
        x = _post_norm(x, 0.5 * ada_x[2] * _swiglu(hx, ffn1_w_gu[l], ffn1_w_down[l]), ln_g[l, 0], ln_b[l, 0])
        ctx = _post_norm(ctx, 0.5 * ada_c[2] * _swiglu(hc, ffn1_w_gu[l], ffn1_w_down[l]), ln_g[l, 0], ln_b[l, 0])
        hx = _modulate(x, ada_x[3], ada_x[4])
        hc = _modulate(ctx, ada_c[3], ada_c[4])
        yx, yc = _mixer(hx, hc, w_in[l], na_rpb[l], cv_conv_w[l], gd_conv_w[l], gd_a_log[l], gd_dt_bias[l],
                        gd_norm_w[l], w_out[l], ang, not last)
        x = _post_norm(x, ada_x[5] * yx, ln_g[l, 1], ln_b[l, 1])
        hx = _modulate(x, ada_x[6], ada_x[7])
        x = _post_norm(x, 0.5 * ada_x[8] * _swiglu(hx, ffn2_w_gu[l], ffn2_w_down[l]), ln_g[l, 2], ln_b[l, 2])
        if not last:
            ctx = _post_norm(ctx, ada_c[5] * yc, ln_g[l, 1], ln_b[l, 1])
            hc = _modulate(ctx, ada_c[6], ada_c[7])
            ctx = _post_norm(ctx, 0.5 * ada_c[8] * _swiglu(hc, ffn2_w_gu[l], ffn2_w_down[l]), ln_g[l, 2], ln_b[l, 2])
    return x
```

```python
import functools
import math

import jax
import jax.numpy as jnp
from jax import lax
from jax.experimental import pallas as pl
from jax.experimental.pallas import tpu as pltpu

F32 = jnp.float32
BF16 = jnp.bfloat16

HEAD_DIM = 128
GRID_W = 64
GD_CHUNK = 128
ROPE_BASE = 10000.0
LN_EPS = 1e-6
NEG_INF = -1e30
N_ADA = 9
V7X_VMEM_LIMIT = 56 * 1024 * 1024


def _cparams(sem):
    return pltpu.CompilerParams(dimension_semantics=sem, vmem_limit_bytes=V7X_VMEM_LIMIT)


def _ln(x):
    mu = jnp.mean(x, axis=-1, keepdims=True)
    xc = x - mu
    var = jnp.mean(xc * xc, axis=-1, keepdims=True)
    return xc * lax.rsqrt(var + LN_EPS)


def _silu(x):
    return x * jax.nn.sigmoid(x)


def _softplus(x):
    return jnp.maximum(x, 0.0) + jnp.log(1.0 + jnp.exp(-jnp.abs(x)))


def _dot(a, b):
    return jnp.dot(a, b, preferred_element_type=F32)


def _dot_nt(a, b):
    return lax.dot_general(a, b, (((1,), (1,)), ((), ())), preferred_element_type=F32)


def _dot_tn(a, b):
    return lax.dot_general(a, b, (((0,), (0,)), ((), ())), preferred_element_type=F32)


def _ada_kernel(c_ref, w_ref, b_ref, o_ref):
    s = _silu(c_ref[...]).astype(BF16)
    o_ref[...] = _dot(s, w_ref[...].astype(BF16)) + b_ref[...]


def _ada_table(cvec, w_ada, b_ada, tn):
    depth, d, n = w_ada.shape
    return pl.pallas_call(
        _ada_kernel,
        grid=(depth, n // tn),
        in_specs=[pl.BlockSpec((8, d), lambda l, j: (0, 0)),
                  pl.BlockSpec((None, d, tn), lambda l, j: (l, 0, j)),
                  pl.BlockSpec((None, 1, tn), lambda l, j: (l, 0, j))],
        out_specs=pl.BlockSpec((None, 8, tn), lambda l, j: (l, 0, j)),
        out_shape=jax.ShapeDtypeStruct((depth, 8, n), F32),
        compiler_params=_cparams(("parallel", "parallel")),
        name="ada_table",
    )(cvec, w_ada, b_ada.reshape(depth, 1, n))


def _ada_spec(d, l, grp, j):
    return pl.BlockSpec((None, None, None, 1, d), lambda i, *_: (l, grp(i), j, 0, 0))


def _ln_spec(d, l, k):
    return pl.BlockSpec((None, None, 1, d), lambda *_: (l, k, 0, 0))


def _ffn_kernel(x_ref, sh_ref, sc_ref, gt_ref, wg_ref, wu_ref, wd_ref, lg_ref, lb_ref, o_ref, h_ref, *, alpha):
    f = pl.program_id(1)

    @pl.when(f == 0)
    def _():
        h = _ln(x_ref[...]) * (1.0 + sc_ref[...]) + sh_ref[...]
        h_ref[...] = h.astype(BF16)
        o_ref[...] = jnp.zeros_like(o_ref)

    h = h_ref[...]
    g = _dot(h, wg_ref[...])
    u = _dot(h, wu_ref[...])
    a = (_silu(g) * u).astype(BF16)
    o_ref[...] += _dot(a, wd_ref[...])

    @pl.when(f == pl.num_programs(1) - 1)
    def _():
        z = alpha * x_ref[...] + (0.5 * gt_ref[...]) * o_ref[...]
        o_ref[...] = _ln(z) * lg_ref[...] + lb_ref[...]


def _ffn(x, ada, ln_g, ln_b, l, j0, k_ln, grp, w_gu, w_down, tm, tf, alpha):
    m, d = x.shape
    d_ff = w_down.shape[1]
    nf = d_ff // tf
    return pl.pallas_call(
        functools.partial(_ffn_kernel, alpha=alpha),
        grid=(m // tm, nf),
        in_specs=[pl.BlockSpec((tm, d), lambda i, f: (i, 0)),
                  _ada_spec(d, l, grp, j0), _ada_spec(d, l, grp, j0 + 1), _ada_spec(d, l, grp, j0 + 2),
                  pl.BlockSpec((None, d, tf), lambda i, f: (l, 0, f)),
                  pl.BlockSpec((None, d, tf), lambda i, f: (l, 0, f + nf)),
                  pl.BlockSpec((None, tf, d), lambda i, f: (l, f, 0)),
                  _ln_spec(d, l, k_ln), _ln_spec(d, l, k_ln)],
        out_specs=pl.BlockSpec((tm, d), lambda i, f: (i, 0)),
        out_shape=jax.ShapeDtypeStruct((m, d), F32),
        scratch_shapes=[pltpu.VMEM((tm, d), BF16)],
        compiler_params=_cparams(("parallel", "arbitrary")),
        name="ffn",
    )(x, ada, ada, ada, w_gu, w_gu, w_down, ln_g, ln_b)


def _inproj_kernel(x_ref, sh_ref, sc_ref, w_ref, o_ref, h_ref):
    @pl.when(pl.program_id(1) == 0)
    def _():
        h = _ln(x_ref[...]) * (1.0 + sc_ref[...]) + sh_ref[...]
        h_ref[...] = h.astype(BF16)

    o_ref[...] = _dot(h_ref[...], w_ref[...]).astype(o_ref.dtype)


def _inproj(x, ada, l, grp, w, col0, width, tn, tm, out_dtype):
    m, d = x.shape
    off = 0 if col0 is None else col0 // tn
    return pl.pallas_call(
        _inproj_kernel,
        grid=(m // tm, width // tn),
        in_specs=[pl.BlockSpec((tm, d), lambda i, n: (i, 0)),
                  _ada_spec(d, l, grp, 3), _ada_spec(d, l, grp, 4),
                  pl.BlockSpec((None, d, tn), lambda i, n: (l, 0, n + off))],
        out_specs=pl.BlockSpec((tm, tn), lambda i, n: (i, n)),
        out_shape=jax.ShapeDtypeStruct((m, width), out_dtype),
        scratch_shapes=[pltpu.VMEM((tm, d), BF16)],
        compiler_params=_cparams(("parallel", "arbitrary")),
        name="inproj",
    )(x, ada, ada, w)


def _softmax_pv(s_parts, v_parts):
    m = functools.reduce(jnp.maximum, [jnp.max(s, axis=-1, keepdims=True) for s in s_parts])
    ps = [jnp.exp(s - m) for s in s_parts]
    den = functools.reduce(lambda a, b: a + b, [jnp.sum(p, axis=-1, keepdims=True) for p in ps])
    o = functools.reduce(lambda a, b: a + b, [_dot(p.astype(BF16), v) for p, v in zip(ps, v_parts)])
    return o / den


def _na_kernel(q_ref, k_ref, v_ref, kc_ref, vc_ref, b_ref, o_ref, *, heads, rows, wr, scale):
    r = pl.program_id(1)
    start = pl.multiple_of(jnp.clip(r - wr // 2, 0, rows - wr) * GRID_W, GRID_W)
    nb = wr * GRID_W
    for h in range(heads):
        cs = slice(h * HEAD_DIM, (h + 1) * HEAD_DIM)
        q = q_ref[:, cs]
        kb = k_ref[pl.ds(start, nb), cs]
        vb = v_ref[pl.ds(start, nb), cs]
        s_loc = _dot_nt(q, kb) * scale + b_ref[h]
        s_ctx = _dot_nt(q, kc_ref[:, cs]) * scale
        o_ref[:, cs] = _softmax_pv([s_loc, s_ctx], [vb, vc_ref[:, cs]]).astype(o_ref.dtype)


def _na_attention(px, pc, bias, b, t, lc, heads, win_r):
    na_d = heads * HEAD_DIM
    rows = t // GRID_W
    wr = min(win_r, rows)

    def dr0(r):
        return jnp.clip(r - wr // 2, 0, rows - wr) - r + win_r - 1

    return pl.pallas_call(
        functools.partial(_na_kernel, heads=heads, rows=rows, wr=wr, scale=HEAD_DIM ** -0.5),
        grid=(b, rows),
        in_specs=[pl.BlockSpec((GRID_W, na_d), lambda i, r: (i * rows + r, 0)),
                  pl.BlockSpec((t, na_d), lambda i, r: (i, 1)),
                  pl.BlockSpec((t, na_d), lambda i, r: (i, 2)),
                  pl.BlockSpec((lc, na_d), lambda i, r: (i, 1)),
                  pl.BlockSpec((lc, na_d), lambda i, r: (i, 2)),
                  pl.BlockSpec((heads, None, GRID_W, wr * GRID_W), lambda i, r: (0, dr0(r), 0, 0))],
        out_specs=pl.BlockSpec((GRID_W, na_d), lambda i, r: (i * rows + r, 0)),
        out_shape=jax.ShapeDtypeStruct((b * t, na_d), BF16),
        compiler_params=_cparams(("parallel", "arbitrary")),
        name="na_attention",
    )(px, px, px, pc, pc, bias)


def _dense_attn_kernel(q_ref, k_ref, v_ref, o_ref, *, heads, scale):
    for h in range(heads):
        cs = slice(h * HEAD_DIM, (h + 1) * HEAD_DIM)
        s = _dot_nt(q_ref[:, cs], k_ref[:, cs]) * scale
        o_ref[:, cs] = _softmax_pv([s], [v_ref[:, cs]]).astype(o_ref.dtype)


def _dense_attention(pc, b, lc, heads):
    na_d = heads * HEAD_DIM
    return pl.pallas_call(
        functools.partial(_dense_attn_kernel, heads=heads, scale=HEAD_DIM ** -0.5),
        grid=(b,),
        in_specs=[pl.BlockSpec((lc, na_d), lambda i: (i, 0)),
                  pl.BlockSpec((lc, na_d), lambda i: (i, 1)),
                  pl.BlockSpec((lc, na_d), lambda i: (i, 2))],
        out_specs=pl.BlockSpec((lc, na_d), lambda i: (i, 0)),
        out_shape=jax.ShapeDtypeStruct((b * lc, na_d), BF16),
        compiler_params=_cparams(("parallel",)),
        name="ctx_attention",
    )(pc, pc, pc)


def _na_bias_table(rpb, rows, win_r, win_c):
    wr = min(win_r, rows)
    cc = jnp.arange(GRID_W)
    col_start = jnp.clip(cc - win_c // 2, 0, GRID_W - win_c)
    col_ok = (cc[None, :] >= col_start[:, None]) & (cc[None, :] < col_start[:, None] + win_c)
    dc = jnp.clip(cc[None, :] - cc[:, None] + win_c - 1, 0, 2 * win_c - 2)
    dr = jnp.arange(win_r)[:, None] + jnp.arange(wr)[None, :]
    tab = rpb[:, dr[:, None, :, None], dc[None, :, None, :]].astype(F32)
    tab = jnp.where(col_ok[None, None, :, None, :], tab, NEG_INF)
    return tab.reshape(rpb.shape[0], win_r, GRID_W, wr * GRID_W)


def _shift_rows(x, k):
    n = x.shape[0]
    rolled = pltpu.roll(x, k % n, 0)
    t = lax.broadcasted_iota(jnp.int32, x.shape, 0)
    ok = (t >= k) if k > 0 else (t < n + k)
    return jnp.where(ok, rolled, 0.0)


def _cv_kernel(b_ref, c_ref, u_ref, w_ref, o_ref):
    cu = c_ref[...] * u_ref[...]
    w = w_ref[...]
    ksz = w.shape[0]
    y = jnp.zeros_like(cu)
    for j in range(ksz):
        sh = ksz // 2 - j
        y = y + w[j:j + 1, :] * (cu if sh == 0 else _shift_rows(cu, sh))
    o_ref[...] = (b_ref[...] * y).astype(o_ref.dtype)


def _short_conv(p, cv_w, l, nseq, tseq, cv_groups):
    ksz = cv_w.shape[1]
    cv_d = cv_groups * HEAD_DIM
    return pl.pallas_call(
        _cv_kernel,
        grid=(nseq, cv_groups),
        in_specs=[pl.BlockSpec((tseq, HEAD_DIM), lambda s, j: (s, j)),
                  pl.BlockSpec((tseq, HEAD_DIM), lambda s, j: (s, cv_groups + j)),
                  pl.BlockSpec((tseq, HEAD_DIM), lambda s, j: (s, 2 * cv_groups + j)),
                  pl.BlockSpec((None, ksz, HEAD_DIM), lambda s, j: (l, 0, j))],
        out_specs=pl.BlockSpec((tseq, HEAD_DIM), lambda s, j: (s, j)),
        out_shape=jax.ShapeDtypeStruct((nseq * tseq, cv_d), BF16),
        compiler_params=_cparams(("parallel", "parallel")),
        name="short_conv",
    )(p, p, p, cv_w)


def _gdn_prep_kernel(x_ref, w_ref, cos_ref, sin_ref, o_ref, *, gd_heads):
    j = pl.program_id(1)
    x = x_ref[...]
    w = w_ref[...]
    ksz = w.shape[0]
    y = jnp.zeros_like(x)
    for i in range(ksz):
        sh = ksz // 2 - i
        y = y + w[i:i + 1, :] * (x if sh == 0 else _shift_rows(x, sh))
    st = _silu(y).T
    ss = jnp.sum(st * st, axis=0, keepdims=True)
    nrm = st * lax.rsqrt(ss + LN_EPS)
    q4 = HEAD_DIM // 4
    swapped = jnp.concatenate([nrm[q4:2 * q4], nrm[0:q4], nrm[3 * q4:4 * q4], nrm[2 * q4:3 * q4]], axis=0)
    rot = nrm * cos_ref[...] + swapped * sin_ref[...]
    rot = rot * jnp.where(j < gd_heads, HEAD_DIM ** -0.5, 1.0)
    res = jnp.where(j < 2 * gd_heads, rot, st)
    for c in range(o_ref.shape[0]):
        o_ref[c] = res[:, c * GD_CHUNK:(c + 1) * GD_CHUNK]


def _gdn_prep(p, conv_w, cos_t, sin_t, l, nseq, tseq, col0_blk, gd_heads):
    ksz = conv_w.shape[1]
    nch = tseq // GD_CHUNK
    return pl.pallas_call(
        functools.partial(_gdn_prep_kernel, gd_heads=gd_heads),
        grid=(nseq, 3 * gd_heads),
        in_specs=[pl.BlockSpec((tseq, HEAD_DIM), lambda s, j: (s, col0_blk + j)),
                  pl.BlockSpec((None, ksz, HEAD_DIM), lambda s, j: (l, 0, j)),
                  pl.BlockSpec((HEAD_DIM, tseq), lambda s, j: (0, 0)),
                  pl.BlockSpec((HEAD_DIM, tseq), lambda s, j: (0, 0))],
        out_specs=pl.BlockSpec((None, nch, HEAD_DIM, GD_CHUNK), lambda s, j: (j, s, 0, 0)),
        out_shape=jax.ShapeDtypeStruct((3 * gd_heads, nseq * nch, HEAD_DIM, GD_CHUNK), F32),
        compiler_params=_cparams(("parallel", "parallel")),
        name="gdn_prep",
    )(p, conv_w, cos_t, sin_t)


def _gdn_gate_kernel(t_ref, alog_ref, dtb_ref, rows_ref, dt_ref, *, gd_heads):
    ndh = 2 * gd_heads
    half = HEAD_DIM // 2
    a = t_ref[...].T
    beta = jax.nn.sigmoid(a)
    g = -jnp.exp(alog_ref[...]) * _softplus(a + dtb_ref[...])
    ri = lax.broadcasted_iota(jnp.int32, (GD_CHUNK, GD_CHUNK), 0)
    ci = lax.broadcasted_iota(jnp.int32, (GD_CHUNK, GD_CHUNK), 1)
    g1 = g.astype(BF16)
    r1 = g - g1.astype(F32)
    g2 = r1.astype(BF16)
    g3 = (r1 - g2.astype(F32)).astype(BF16)

    def cumsum(tri):
        tb = jnp.where(tri, 1.0, 0.0).astype(BF16)
        return _dot(g1, tb) + _dot(g2, tb) + _dot(g3, tb)

    gc = jnp.where(ri < half + gd_heads, cumsum(ri <= ci), cumsum(ri >= ci))
    glast = jnp.sum(g, axis=1, keepdims=True)
    e = jnp.exp(gc)
    f = jnp.exp(glast - gc)
    gt = jnp.exp(glast) + jnp.zeros_like(gc)
    gct = gc.T
    zero_rows = jnp.zeros((3, GD_CHUNK), F32)
    for dh in range(ndh):
        r = half + dh
        rows_ref[dh, 0:1, :] = beta[dh:dh + 1]
        rows_ref[dh, 1:2, :] = beta[dh:dh + 1] * e[r:r + 1]
        rows_ref[dh, 2:3, :] = f[r:r + 1]
        rows_ref[dh, 3:4, :] = gt[r:r + 1]
        rows_ref[dh, 4:5, :] = e[r:r + 1]
        rows_ref[dh, 5:8, :] = zero_rows
        col = jnp.sum(jnp.where(ci == r, gct, 0.0), axis=1, keepdims=True)
        diff = gc[r:r + 1] - col
        mask = (ci >= ri) if dh < gd_heads else (ci <= ri)
        dt_ref[dh] = jnp.where(mask, jnp.exp(jnp.where(mask, diff, 0.0)), 0.0)


def _gdn_gates(tail, alog_t, dtb_t, gd_heads):
    m = tail.shape[0]
    nch = m // GD_CHUNK
    ndh = 2 * gd_heads
    return pl.pallas_call(
        functools.partial(_gdn_gate_kernel, gd_heads=gd_heads),
        grid=(nch,),
        in_specs=[pl.BlockSpec((GD_CHUNK, HEAD_DIM), lambda n: (n, 0)),
                  pl.BlockSpec((HEAD_DIM, GD_CHUNK), lambda n: (0, 0)),
                  pl.BlockSpec((HEAD_DIM, GD_CHUNK), lambda n: (0, 0))],
        out_specs=[pl.BlockSpec((None, ndh, 8, GD_CHUNK), lambda n: (n, 0, 0, 0)),
                   pl.BlockSpec((None, ndh, GD_CHUNK, GD_CHUNK), lambda n: (n, 0, 0, 0))],
        out_shape=[jax.ShapeDtypeStruct((nch, ndh, 8, GD_CHUNK), F32),
                   jax.ShapeDtypeStruct((nch, ndh, GD_CHUNK, GD_CHUNK), F32)],
        compiler_params=_cparams(("parallel",)),
        name="gdn_gates",
    )(tail, alog_t, dtb_t)


def _tri_inverse_full(mneg, ri, ci):
    eye = jnp.where(ri == ci, 1.0, 0.0)

    def blk(n):
        return (ri // n) == (ci // n)

    q = jnp.where(blk(16), mneg, 0.0)
    x = eye + q
    q = _dot(q.astype(BF16), q.astype(BF16))
    for _ in range(2):
        qb = q.astype(BF16)
        x = x + _dot(x.astype(BF16), qb)
        q = _dot(qb, qb)
    x = x + _dot(x.astype(BF16), q.astype(BF16))
    for n in (32, 64, 128):
        off = jnp.where(blk(n) & jnp.logical_not(blk(n // 2)), mneg, 0.0)
        xb = x.astype(BF16)
        x = x + _dot(_dot(xb, off.astype(BF16)).astype(BF16), xb)
    return x


def _gdn_kernel(qx, kx, vx, qc, kc, vc, rx0, rx1, rc0, rc1, dx0, dx1, dc0, dc1, zx, zc, nw,
                ox, oc, ut_s, wq_s, at_s, kf_s, o_s, *, ncc, ncx):
    nc = ncc + ncx
    ri = lax.broadcasted_iota(jnp.int32, (GD_CHUNK, GD_CHUNK), 0)
    ci = lax.broadcasted_iota(jnp.int32, (GD_CHUNK, GD_CHUNK), 1)
    strict = (ci > ri, ci < ri)

    def local(i, n, q_ref, k_ref, v_ref, rows, dts):
        kt = k_ref[i]
        qt = q_ref[i]
        vt = v_ref[i]
        kb = kt.astype(BF16)
        kk = _dot_tn(kb, kb)
        qk = _dot_tn(kb, qt.astype(BF16))
        for d in range(2):
            rw = rows[d][i]
            beta, beta_e, f, e = rw[0:1], rw[1:2], rw[2:3], rw[4:5]
            dt = dts[d][i]
            mneg = -(jnp.where(strict[d], dt, 0.0) * kk * beta)
            tt = _tri_inverse_full(mneg, ri, ci)
            rhs = jnp.concatenate([vt * beta, kt * beta_e], axis=0).astype(BF16)
            sol = _dot(rhs, tt.astype(BF16))
            ut_s[d, n] = sol[:HEAD_DIM]
            wq_s[d, n, :, 0:GD_CHUNK] = sol[HEAD_DIM:].astype(BF16)
            wq_s[d, n, :, GD_CHUNK:2 * GD_CHUNK] = (qt * e).astype(BF16)
            at_s[d, n] = (qk * dt).astype(BF16)
            kf_s[d, n] = (kt * f).astype(BF16)
        o_s[n] = jnp.zeros((HEAD_DIM, GD_CHUNK), F32)

    def seg_local(q_ref, k_ref, v_ref, rows, dts, count, base):
        def body(i, carry):
            local(i, base + i, q_ref, k_ref, v_ref, rows, dts)
            return carry
        lax.fori_loop(0, count, body, 0)

    seg_local(qc, kc, vc, (rc0, rc1), (dc0, dc1), ncc, 0)
    seg_local(qx, kx, vx, (rx0, rx1), (dx0, dx1), ncx, ncc)

    def step(st, d, n, gt):
        y = _dot(st.astype(BF16), wq_s[d, n])
        vn = ut_s[d, n] - y[:, 0:GD_CHUNK]
        vnb = vn.astype(BF16)
        o_s[n] += y[:, GD_CHUNK:2 * GD_CHUNK] + _dot(vnb, at_s[d, n])
        return st * gt + _dot_nt(vnb, kf_s[d, n])

    def seg_scan(states, rows, count, base):
        def body(i, carry):
            s0, s1 = carry
            j = count - 1 - i
            s0 = step(s0, 0, base + i, rows[0][i][3:4])
            s1 = step(s1, 1, base + j, rows[1][j][3:4])
            return s0, s1
        return lax.fori_loop(0, count, body, states)

    zero = jnp.zeros((HEAD_DIM, HEAD_DIM), F32)
    states = seg_scan((zero, zero), (rc0, rc1), ncc, 0)
    seg_scan(states, (rx0, rx1), ncx, ncc)

    def seg_out(o_ref, z_ref, count, base):
        def body(i, carry):
            ot = o_s[base + i]
            ms = jnp.mean(ot * ot, axis=0, keepdims=True)
            on = (ot * lax.rsqrt(ms + LN_EPS)).T
            rs = pl.ds(pl.multiple_of(i * GD_CHUNK, GD_CHUNK), GD_CHUNK)
            o_ref[rs, :] = (on * nw[...] * _silu(z_ref[rs, :])).astype(o_ref.dtype)
            return carry
        lax.fori_loop(0, count, body, 0)

    seg_out(oc, zc, ncc, 0)
    seg_out(ox, zx, ncx, ncc)


def _gdn_scan(gt_x, gt_c, rows_x, rows_c, dt_x, dt_c, px, pc, norm_w, l, b, t, lc, gd_heads, z_blk):
    ncx, ncc = t // GD_CHUNK, lc // GD_CHUNK
    nc = ncx + ncc
    gd_d = gd_heads * HEAD_DIM

    def tile_spec(n, which):
        return pl.BlockSpec((None, n, HEAD_DIM, GD_CHUNK), lambda i, h: (which * gd_heads + h, i, 0, 0))

    def row_spec(n, d):
        return pl.BlockSpec((n, None, 8, GD_CHUNK), lambda i, h: (i, d * gd_heads + h, 0, 0))

    def dt_spec(n, d):
        return pl.BlockSpec((n, None, GD_CHUNK, GD_CHUNK), lambda i, h: (i, d * gd_heads + h, 0, 0))

    return pl.pallas_call(
        functools.partial(_gdn_kernel, ncc=ncc, ncx=ncx),
        grid=(b, gd_heads),
        in_specs=[tile_spec(ncx, 0), tile_spec(ncx, 1), tile_spec(ncx, 2),
                  tile_spec(ncc, 0), tile_spec(ncc, 1), tile_spec(ncc, 2),
                  row_spec(ncx, 0), row_spec(ncx, 1), row_spec(ncc, 0), row_spec(ncc, 1),
                  dt_spec(ncx, 0), dt_spec(ncx, 1), dt_spec(ncc, 0), dt_spec(ncc, 1),
                  pl.BlockSpec((t, HEAD_DIM), lambda i, h: (i, z_blk + h)),
                  pl.BlockSpec((lc, HEAD_DIM), lambda i, h: (i, z_blk + h)),
                  pl.BlockSpec((None, 1, HEAD_DIM), lambda i, h: (l, 0, 0))],
        out_specs=[pl.BlockSpec((t, HEAD_DIM), lambda i, h: (i, h)),
                   pl.BlockSpec((lc, HEAD_DIM), lambda i, h: (i, h))],
        out_shape=[jax.ShapeDtypeStruct((b * t, gd_d), BF16),
                   jax.ShapeDtypeStruct((b * lc, gd_d), BF16)],
        scratch_shapes=[pltpu.VMEM((2, nc, HEAD_DIM, GD_CHUNK), F32),
                        pltpu.VMEM((2, nc, HEAD_DIM, 2 * GD_CHUNK), BF16),
                        pltpu.VMEM((2, nc, GD_CHUNK, GD_CHUNK), BF16),
                        pltpu.VMEM((2, nc, HEAD_DIM, GD_CHUNK), BF16),
                        pltpu.VMEM((nc, HEAD_DIM, GD_CHUNK), F32)],
        compiler_params=_cparams(("parallel", "arbitrary")),
        name="gdn_scan",
    )(gt_x, gt_x, gt_x, gt_c, gt_c, gt_c, rows_x, rows_x, rows_c, rows_c, dt_x, dt_x, dt_c, dt_c,
      px, pc, norm_w)


def _outproj_kernel(x_ref, na_ref, cv_ref, gd_ref, w_ref, gt_ref, lg_ref, lb_ref, o_ref, *, alpha):
    n0 = na_ref.shape[1]
    n1 = n0 + cv_ref.shape[1]
    y = (_dot(na_ref[...], w_ref[0:n0, :]) + _dot(cv_ref[...], w_ref[n0:n1, :])
         + _dot(gd_ref[...], w_ref[n1:, :]))
    z = alpha * x_ref[...] + gt_ref[...] * y
    o_ref[...] = _ln(z) * lg_ref[...] + lb_ref[...]


def _outproj(x, o_na, o_cv, o_gd, w_out, ada, ln_g, ln_b, l, grp, tm, alpha):
    m, d = x.shape
    d_mix = w_out.shape[1]
    return pl.pallas_call(
        functools.partial(_outproj_kernel, alpha=alpha),
        grid=(m // tm,),
        in_specs=[pl.BlockSpec((tm, d), lambda i: (i, 0)),
                  pl.BlockSpec((tm, o_na.shape[1]), lambda i: (i, 0)),
                  pl.BlockSpec((tm, o_cv.shape[1]), lambda i: (i, 0)),
                  pl.BlockSpec((tm, o_gd.shape[1]), lambda i: (i, 0)),
                  pl.BlockSpec((None, d_mix, d), lambda i: (l, 0, 0)),
                  _ada_spec(d, l, grp, 5), _ln_spec(d, l, 1), _ln_spec(d, l, 1)],
        out_specs=pl.BlockSpec((tm, d), lambda i: (i, 0)),
        out_shape=jax.ShapeDtypeStruct((m, d), F32),
        compiler_params=_cparams(("parallel",)),
        name="outproj",
    )(x, o_na, o_cv, o_gd, w_out, ada, ln_g, ln_b)


def _rope_tables(t):
    tok = jnp.arange(t, dtype=jnp.int32)
    row = (tok // GRID_W).astype(F32)
    col = (tok % GRID_W).astype(F32)
    n_freq = HEAD_DIM // 4
    inv_freq = ROPE_BASE ** (-jnp.arange(n_freq, dtype=F32) / n_freq)
    ar = (row[:, None] * inv_freq).T
    ac = (col[:, None] * inv_freq).T
    cos_t = jnp.concatenate([jnp.cos(ar), jnp.cos(ar), jnp.cos(ac), jnp.cos(ac)], axis=0)
    sin_t = jnp.concatenate([-jnp.sin(ar), jnp.sin(ar), -jnp.sin(ac), jnp.sin(ac)], axis=0)
    return cos_t, sin_t


def kernel(x, c, ctx, c_ctx, w_ada, b_ada, ln_g, ln_b, ffn1_w_gu, ffn1_w_down, w_in, na_rpb, cv_conv_w,
           gd_conv_w, gd_a_log, gd_dt_bias, gd_norm_w, w_out, ffn2_w_gu, ffn2_w_down):
    b, t, d = x.shape
    lc = ctx.shape[1]
    depth = w_ada.shape[0]
    d_ff = ffn1_w_down.shape[1]
    na_heads = na_rpb.shape[1]
    win_r, win_c = (na_rpb.shape[2] + 1) // 2, (na_rpb.shape[3] + 1) // 2
    cv_d = cv_conv_w.shape[2]
    cv_groups = cv_d // HEAD_DIM
    gd_heads = gd_a_log.shape[2]
    na_d, gd_d = na_heads * HEAD_DIM, gd_heads * HEAD_DIM
    ndh = 2 * gd_heads
    n_main = 3 * na_d + 3 * cv_d + 3 * gd_d + gd_d
    alpha = (2 * depth) ** 0.25
    assert b + 1 <= 8 and t % GD_CHUNK == 0 and lc % GD_CHUNK == 0 and ndh <= HEAD_DIM // 2
    assert t // GRID_W >= win_r

    tm = math.gcd(math.gcd(t, b * lc), 512)
    tf = 512 if d_ff % 512 == 0 else 128
    tn_ada = 1024 if (N_ADA * d) % 1024 == 0 else 128
    tn_in = math.gcd(math.gcd(3 * na_d, 3 * cv_d), math.gcd(gd_d, 768))
    rest_w = n_main - 3 * na_d

    bf = lambda w: w.astype(BF16)
    w_gu1, w_dn1, w_gu2, w_dn2, w_in_b, w_out_b = map(bf, (ffn1_w_gu, ffn1_w_down, ffn2_w_gu, ffn2_w_down, w_in, w_out))
    half = HEAD_DIM // 2
    w_tail = jnp.zeros((depth, d, HEAD_DIM), BF16)
    w_tail = w_tail.at[:, :, 0:ndh].set(w_in_b[:, :, n_main:n_main + ndh])
    w_tail = w_tail.at[:, :, half:half + ndh].set(w_in_b[:, :, n_main + ndh:n_main + 2 * ndh])
    alog_t = jnp.zeros((depth, HEAD_DIM, GD_CHUNK), F32).at[:, half:half + ndh, :].set(
        jnp.broadcast_to(gd_a_log.reshape(depth, ndh, 1), (depth, ndh, GD_CHUNK)))
    dtb_t = jnp.zeros((depth, HEAD_DIM, GD_CHUNK), F32).at[:, half:half + ndh, :].set(
        jnp.broadcast_to(gd_dt_bias.reshape(depth, ndh, 1), (depth, ndh, GD_CHUNK)))
    ln_g4 = ln_g.reshape(depth, 3, 1, d)
    ln_b4 = ln_b.reshape(depth, 3, 1, d)
    norm_w3 = gd_norm_w.reshape(depth, 1, HEAD_DIM)
    cos_x, sin_x = _rope_tables(t)
    cos_c, sin_c = jnp.ones((HEAD_DIM, lc), F32), jnp.zeros((HEAD_DIM, lc), F32)

    cvec = jnp.zeros((8, d), F32).at[0:b].set(c).at[b].set(c_ctx)
    ada = _ada_table(cvec, w_ada, b_ada, tn_ada).reshape(depth, 8, N_ADA, 1, d)

    grp_x = lambda i: (i * tm) // t
    grp_c = lambda i: b

    xs = x.reshape(b * t, d)
    cs = ctx.reshape(b * lc, d)
    cv_blk = 0
    gd_blk = 3 * cv_groups
    z_blk = gd_blk + 3 * gd_heads

    for l in range(depth):
        last = l == depth - 1
        xs = _ffn(xs, ada, ln_g4, ln_b4, l, 0, 0, grp_x, w_gu1, w_dn1, tm, tf, alpha)
        cs = _ffn(cs, ada, ln_g4, ln_b4, l, 0, 0, grp_c, w_gu1, w_dn1, tm, tf, alpha)

        pna_x = _inproj(xs, ada, l, grp_x, w_in_b, 0, 3 * na_d, tn_in, tm, BF16)
        pna_c = _inproj(cs, ada, l, grp_c, w_in_b, 0, 3 * na_d, tn_in, tm, BF16)
        pr_x = _inproj(xs, ada, l, grp_x, w_in_b, 3 * na_d, rest_w, tn_in, tm, F32)
        pr_c = _inproj(cs, ada, l, grp_c, w_in_b, 3 * na_d, rest_w, tn_in, tm, F32)
        tl_x = _inproj(xs, ada, l, grp_x, w_tail, None, HEAD_DIM, HEAD_DIM, tm, F32)
        tl_c = _inproj(cs, ada, l, grp_c, w_tail, None, HEAD_DIM, HEAD_DIM, tm, F32)

        bias = _na_bias_table(na_rpb[l], t // GRID_W, win_r, win_c)
        o_na_x = _na_attention(pna_x, pna_c, bias, b, t, lc, na_heads, win_r)
        o_cv_x = _short_conv(pr_x, cv_conv_w, l, b, t, cv_groups)

        rows_x, dt_x = _gdn_gates(tl_x, alog_t[l], dtb_t[l], gd_heads)
        rows_c, dt_c = _gdn_gates(tl_c, alog_t[l], dtb_t[l], gd_heads)
        gt_x = _gdn_prep(pr_x, gd_conv_w, cos_x, sin_x, l, b, t, gd_blk, gd_heads)
        gt_c = _gdn_prep(pr_c, gd_conv_w, cos_c, sin_c, l, b, lc, gd_blk, gd_heads)
        o_gd_x, o_gd_c = _gdn_scan(gt_x, gt_c, rows_x, rows_c, dt_x, dt_c, pr_x, pr_c, norm_w3, l,
                                   b, t, lc, gd_heads, z_blk)

        xs = _outproj(xs, o_na_x, o_cv_x, o_gd_x, w_out_b, ada, ln_g4, ln_b4, l, grp_x, tm, alpha)
        xs = _ffn(xs, ada, ln_g4, ln_b4, l, 6, 2, grp_x, w_gu2, w_dn2, tm, tf, alpha)
        if not last:
            o_na_c = _dense_attention(pna_c, b, lc, na_heads)
            o_cv_c = _short_conv(pr_c, cv_conv_w, l, b, lc, cv_groups)
            cs = _outproj(cs, o_na_c, o_cv_c, o_gd_c, w_out_b, ada, ln_g4, ln_b4, l, grp_c, tm, alpha)
            cs = _ffn(cs, ada, ln_g4, ln_b4, l, 6, 2, grp_c, w_gu2, w_dn2, tm, tf, alpha)
    return xs.reshape(b, t, d)
```

```python
import functools
import math

import jax
import jax.numpy as jnp
from jax import lax
from jax.experimental import pallas as pl
from jax.experimental.pallas import tpu as pltpu

F32 = jnp.float32
BF16 = jnp.bfloat16

HEAD_DIM = 128
GRID_W = 64
GD_CHUNK = 128
ROPE_BASE = 10000.0
LN_EPS = 1e-6
NEG_INF = -1e30
N_ADA = 9
V7X_VMEM_LIMIT = 56 * 1024 * 1024


def _cparams(sem):
    return pltpu.CompilerParams(dimension_semantics=sem, vmem_limit_bytes=V7X_VMEM_LIMIT)


def _ln(x):
    mu = jnp.mean(x, axis=-1, keepdims=True)
    xc = x - mu
    var = jnp.mean(xc * xc, axis=-1, keepdims=True)
    return xc * lax.rsqrt(var + LN_EPS)


def _silu(x):
    return x * jax.nn.sigmoid(x)


def _softplus(x):
    return jnp.maximum(x, 0.0) + jnp.log(1.0 + jnp.exp(-jnp.abs(x)))


def _dot(a, b):
    return jnp.dot(a, b, preferred_element_type=F32)


def _dot_nt(a, b):
    return lax.dot_general(a, b, (((1,), (1,)), ((), ())), preferred_element_type=F32)


def _dot_tn(a, b):
    return lax.dot_general(a, b, (((0,), (0,)), ((), ())), preferred_element_type=F32)


def _ada_kernel(c_ref, w_ref, b_ref, o_ref):
    s = _silu(c_ref[...]).astype(BF16)
    o_ref[...] = _dot(s, w_ref[...].astype(BF16)) + b_ref[...]


def _ada_table(cvec, w_ada, b_ada, tn):
    depth, d, n = w_ada.shape
    return pl.pallas_call(
        _ada_kernel,
        grid=(depth, n // tn),
        in_specs=[pl.BlockSpec((8, d), lambda l, j: (0, 0)),
                  pl.BlockSpec((None, d, tn), lambda l, j: (l, 0, j)),
                  pl.BlockSpec((None, 1, tn), lambda l, j: (l, 0, j))],
        out_specs=pl.BlockSpec((None, 8, tn), lambda l, j: (l, 0, j)),
        out_shape=jax.ShapeDtypeStruct((depth, 8, n), F32),
        compiler_params=_cparams(("parallel", "parallel")),
        name="ada_table",
    )(cvec, w_ada, b_ada.reshape(depth, 1, n))


def _ada_spec(d, l, grp, j):
    return pl.BlockSpec((None, None, None, 1, d), lambda i, *_: (l, grp(i), j, 0, 0))


def _ln_spec(d, l, k):
    return pl.BlockSpec((None, None, 1, d), lambda *_: (l, k, 0, 0))


def _ffn_kernel(x_ref, sh_ref, sc_ref, gt_ref, wg_ref, wu_ref, wd_ref, lg_ref, lb_ref, o_ref, h_ref, *, alpha):
    f = pl.program_id(1)

    @pl.when(f == 0)
    def _():
        h = _ln(x_ref[...]) * (1.0 + sc_ref[...]) + sh_ref[...]
        h_ref[...] = h.astype(BF16)
        o_ref[...] = jnp.zeros_like(o_ref)

    h = h_ref[...]
    g = _dot(h, wg_ref[...])
    u = _dot(h, wu_ref[...])
    a = (_silu(g) * u).astype(BF16)
    o_ref[...] += _dot(a, wd_ref[...])

    @pl.when(f == pl.num_programs(1) - 1)
    def _():
        z = alpha * x_ref[...] + (0.5 * gt_ref[...]) * o_ref[...]
        o_ref[...] = _ln(z) * lg_ref[...] + lb_ref[...]


def _ffn(x, ada, ln_g, ln_b, l, j0, k_ln, grp, w_gu, w_down, tm, tf, alpha):
    m, d = x.shape
    d_ff = w_down.shape[1]
    nf = d_ff // tf
    return pl.pallas_call(
        functools.partial(_ffn_kernel, alpha=alpha),
        grid=(m // tm, nf),
        in_specs=[pl.BlockSpec((tm, d), lambda i, f: (i, 0)),
                  _ada_spec(d, l, grp, j0), _ada_spec(d, l, grp, j0 + 1), _ada_spec(d, l, grp, j0 + 2),
                  pl.BlockSpec((None, d, tf), lambda i, f: (l, 0, f)),
                  pl.BlockSpec((None, d, tf), lambda i, f: (l, 0, f + nf)),
                  pl.BlockSpec((None, tf, d), lambda i, f: (l, f, 0)),
                  _ln_spec(d, l, k_ln), _ln_spec(d, l, k_ln)],
        out_specs=pl.BlockSpec((tm, d), lambda i, f: (i, 0)),
        out_shape=jax.ShapeDtypeStruct((m, d), F32),
        scratch_shapes=[pltpu.VMEM((tm, d), BF16)],
        compiler_params=_cparams(("parallel", "arbitrary")),
        name="ffn",
    )(x, ada, ada, ada, w_gu, w_gu, w_down, ln_g, ln_b)


def _inproj_kernel(x_ref, sh_ref, sc_ref, w_ref, wt_ref, o_ref, t_ref, h_ref):
    @pl.when(pl.program_id(1) == 0)
    def _():
        h = _ln(x_ref[...]) * (1.0 + sc_ref[...]) + sh_ref[...]
        h_ref[...] = h.astype(BF16)
        t_ref[...] = _dot(h_ref[...], wt_ref[...])

    o_ref[...] = _dot(h_ref[...], w_ref[...].astype(BF16)).astype(o_ref.dtype)


def _inproj(x, ada, l, grp, w_in, w_tail, n_main, tn, tm):
    m, d = x.shape
    return pl.pallas_call(
        _inproj_kernel,
        grid=(m // tm, n_main // tn),
        in_specs=[pl.BlockSpec((tm, d), lambda i, n: (i, 0)),
                  _ada_spec(d, l, grp, 3), _ada_spec(d, l, grp, 4),
                  pl.BlockSpec((None, d, tn), lambda i, n: (l, 0, n)),
                  pl.BlockSpec((None, d, HEAD_DIM), lambda i, n: (l, 0, 0))],
        out_specs=[pl.BlockSpec((tm, tn), lambda i, n: (i, n)),
                   pl.BlockSpec((tm, HEAD_DIM), lambda i, n: (i, 0))],
        out_shape=[jax.ShapeDtypeStruct((m, n_main), BF16),
                   jax.ShapeDtypeStruct((m, HEAD_DIM), F32)],
        scratch_shapes=[pltpu.VMEM((tm, d), BF16)],
        compiler_params=_cparams(("parallel", "arbitrary")),
        name="inproj",
    )(x, ada, ada, w_in, w_tail)


def _softmax_pv(s_parts, v_parts):
    m = functools.reduce(jnp.maximum, [jnp.max(s, axis=-1, keepdims=True) for s in s_parts])
    ps = [jnp.exp(s - m) for s in s_parts]
    den = functools.reduce(lambda a, b: a + b, [jnp.sum(p, axis=-1, keepdims=True) for p in ps])
    o = functools.reduce(lambda a, b: a + b, [_dot(p.astype(BF16), v) for p, v in zip(ps, v_parts)])
    return o / den


def _na_kernel(q_ref, k_ref, v_ref, kc_ref, vc_ref, b_ref, o_ref, *, heads, rows, wr, scale):
    r = pl.program_id(1)
    start = pl.multiple_of(jnp.clip(r - wr // 2, 0, rows - wr) * GRID_W, GRID_W)
    nb = wr * GRID_W

    def scores(h):
        cs = slice(h * HEAD_DIM, (h + 1) * HEAD_DIM)
        q = q_ref[:, cs]
        s_loc = _dot_nt(q, k_ref[pl.ds(start, nb), cs]) * scale + b_ref[h]
        s_ctx = _dot_nt(q, kc_ref[:, cs]) * scale
        return [s_loc, s_ctx]

    pending = scores(0)
    for h in range(heads):
        nxt = scores(h + 1) if h + 1 < heads else None
        cs = slice(h * HEAD_DIM, (h + 1) * HEAD_DIM)
        o_ref[:, cs] = _softmax_pv(pending, [v_ref[pl.ds(start, nb), cs], vc_ref[:, cs]]).astype(o_ref.dtype)
        pending = nxt


def _na_attention(px, pc, bias, l, b, t, lc, heads, win_r):
    na_d = heads * HEAD_DIM
    rows = t // GRID_W
    wr = min(win_r, rows)

    def dr0(r):
        return jnp.clip(r - wr // 2, 0, rows - wr) - r + win_r - 1

    return pl.pallas_call(
        functools.partial(_na_kernel, heads=heads, rows=rows, wr=wr, scale=HEAD_DIM ** -0.5),
        grid=(b, rows),
        in_specs=[pl.BlockSpec((GRID_W, na_d), lambda i, r: (i * rows + r, 0)),
                  pl.BlockSpec((t, na_d), lambda i, r: (i, 1)),
                  pl.BlockSpec((t, na_d), lambda i, r: (i, 2)),
                  pl.BlockSpec((lc, na_d), lambda i, r: (i, 1)),
                  pl.BlockSpec((lc, na_d), lambda i, r: (i, 2)),
                  pl.BlockSpec((None, heads, None, GRID_W, wr * GRID_W), lambda i, r: (l, 0, dr0(r), 0, 0))],
        out_specs=pl.BlockSpec((GRID_W, na_d), lambda i, r: (i * rows + r, 0)),
        out_shape=jax.ShapeDtypeStruct((b * t, na_d), BF16),
        compiler_params=_cparams(("parallel", "arbitrary")),
        name="na_attention",
    )(px, px, px, pc, pc, bias)


def _dense_attn_kernel(q_ref, k_ref, v_ref, o_ref, *, heads, scale):
    for h in range(heads):
        cs = slice(h * HEAD_DIM, (h + 1) * HEAD_DIM)
        s = _dot_nt(q_ref[:, cs], k_ref[:, cs]) * scale
        o_ref[:, cs] = _softmax_pv([s], [v_ref[:, cs]]).astype(o_ref.dtype)


def _dense_attention(pc, b, lc, heads):
    na_d = heads * HEAD_DIM
    return pl.pallas_call(
        functools.partial(_dense_attn_kernel, heads=heads, scale=HEAD_DIM ** -0.5),
        grid=(b,),
        in_specs=[pl.BlockSpec((lc, na_d), lambda i: (i, 0)),
                  pl.BlockSpec((lc, na_d), lambda i: (i, 1)),
                  pl.BlockSpec((lc, na_d), lambda i: (i, 2))],
        out_specs=pl.BlockSpec((lc, na_d), lambda i: (i, 0)),
        out_shape=jax.ShapeDtypeStruct((b * lc, na_d), BF16),
        compiler_params=_cparams(("parallel",)),
        name="ctx_attention",
    )(pc, pc, pc)


def _na_bias_kernel(r_ref, o_ref, *, win_r, win_c, wr):
    n_dr, n_dc = 2 * win_r - 1, 2 * win_c - 1
    base = (pl.program_id(0) * pl.num_programs(1) + pl.program_id(1)) * (n_dr * n_dc)
    q = lax.broadcasted_iota(jnp.int32, (GRID_W, GRID_W), 0)
    k = lax.broadcasted_iota(jnp.int32, (GRID_W, GRID_W), 1)
    dc = jnp.clip(k - q + win_c - 1, 0, n_dc - 1)
    col_start = jnp.clip(q - win_c // 2, 0, GRID_W - win_c)
    col_ok = (k >= col_start) & (k < col_start + win_c)
    for a in range(n_dr):
        acc = jnp.zeros((GRID_W, GRID_W), F32)
        for j in range(n_dc):
            acc = jnp.where(dc == j, r_ref[base + a * n_dc + j], acc)
        g = jnp.where(col_ok, acc, NEG_INF)
        for dr0 in range(win_r):
            w = a - dr0
            if 0 <= w < wr:
                o_ref[dr0, :, w * GRID_W:(w + 1) * GRID_W] = g


def _na_bias_table(rpb, rows, win_r, win_c):
    depth, heads = rpb.shape[0], rpb.shape[1]
    wr = min(win_r, rows)
    return pl.pallas_call(
        functools.partial(_na_bias_kernel, win_r=win_r, win_c=win_c, wr=wr),
        grid=(depth, heads),
        in_specs=[pl.BlockSpec(memory_space=pltpu.SMEM)],
        out_specs=pl.BlockSpec((None, None, win_r, GRID_W, wr * GRID_W), lambda l, h: (l, h, 0, 0, 0)),
        out_shape=jax.ShapeDtypeStruct((depth, heads, win_r, GRID_W, wr * GRID_W), F32),
        compiler_params=_cparams(("parallel", "parallel")),
        name="na_bias",
    )(rpb.reshape(-1))


def _shift_rows(x, k):
    n = x.shape[0]
    rolled = pltpu.roll(x, k % n, 0)
    t = lax.broadcasted_iota(jnp.int32, x.shape, 0)
    ok = (t >= k) if k > 0 else (t < n + k)
    return jnp.where(ok, rolled, 0.0)


def _cv_kernel(b_ref, c_ref, u_ref, w_ref, o_ref):
    cu = c_ref[...].astype(F32) * u_ref[...].astype(F32)
    w = w_ref[...]
    ksz = w.shape[0]
    y = jnp.zeros_like(cu)
    for j in range(ksz):
        sh = ksz // 2 - j
        y = y + w[j:j + 1, :] * (cu if sh == 0 else _shift_rows(cu, sh))
    o_ref[...] = (b_ref[...].astype(F32) * y).astype(o_ref.dtype)


def _short_conv(p, cv_w, l, nseq, tseq, cv_groups, col0_blk):
    ksz = cv_w.shape[1]
    cv_d = cv_groups * HEAD_DIM
    return pl.pallas_call(
        _cv_kernel,
        grid=(nseq, cv_groups),
        in_specs=[pl.BlockSpec((tseq, HEAD_DIM), lambda s, j: (s, col0_blk + j)),
                  pl.BlockSpec((tseq, HEAD_DIM), lambda s, j: (s, col0_blk + cv_groups + j)),
                  pl.BlockSpec((tseq, HEAD_DIM), lambda s, j: (s, col0_blk + 2 * cv_groups + j)),
                  pl.BlockSpec((None, ksz, HEAD_DIM), lambda s, j: (l, 0, j))],
        out_specs=pl.BlockSpec((tseq, HEAD_DIM), lambda s, j: (s, j)),
        out_shape=jax.ShapeDtypeStruct((nseq * tseq, cv_d), BF16),
        compiler_params=_cparams(("parallel", "parallel")),
        name="short_conv",
    )(p, p, p, cv_w)


def _gdn_prep_kernel(x_ref, w_ref, cos_ref, sin_ref, o_ref, *, gd_heads):
    j = pl.program_id(1)
    x = x_ref[...].astype(F32)
    w = w_ref[...]
    ksz = w.shape[0]
    y = jnp.zeros_like(x)
    for i in range(ksz):
        sh = ksz // 2 - i
        y = y + w[i:i + 1, :] * (x if sh == 0 else _shift_rows(x, sh))
    st = _silu(y).T
    ss = jnp.sum(st * st, axis=0, keepdims=True)
    nrm = st * lax.rsqrt(ss + LN_EPS)
    q4 = HEAD_DIM // 4
    swapped = jnp.concatenate([nrm[q4:2 * q4], nrm[0:q4], nrm[3 * q4:4 * q4], nrm[2 * q4:3 * q4]], axis=0)
    rot = nrm * cos_ref[...] + swapped * sin_ref[...]
    rot = rot * jnp.where(j < gd_heads, HEAD_DIM ** -0.5, 1.0)
    res = jnp.where(j < 2 * gd_heads, rot, st)
    for c in range(o_ref.shape[0]):
        o_ref[c] = res[:, c * GD_CHUNK:(c + 1) * GD_CHUNK]


def _gdn_prep(p, conv_w, cos_t, sin_t, l, nseq, tseq, col0_blk, gd_heads):
    ksz = conv_w.shape[1]
    nch = tseq // GD_CHUNK
    return pl.pallas_call(
        functools.partial(_gdn_prep_kernel, gd_heads=gd_heads),
        grid=(nseq, 3 * gd_heads),
        in_specs=[pl.BlockSpec((tseq, HEAD_DIM), lambda s, j: (s, col0_blk + j)),
                  pl.BlockSpec((None, ksz, HEAD_DIM), lambda s, j: (l, 0, j)),
                  pl.BlockSpec((HEAD_DIM, tseq), lambda s, j: (0, 0)),
                  pl.BlockSpec((HEAD_DIM, tseq), lambda s, j: (0, 0))],
        out_specs=pl.BlockSpec((None, nch, HEAD_DIM, GD_CHUNK), lambda s, j: (j, s, 0, 0)),
        out_shape=jax.ShapeDtypeStruct((3 * gd_heads, nseq * nch, HEAD_DIM, GD_CHUNK), F32),
        compiler_params=_cparams(("parallel", "parallel")),
        name="gdn_prep",
    )(p, conv_w, cos_t, sin_t)


def _gdn_gate_kernel(t_ref, alog_ref, dtb_ref, rows_ref, dt_ref, *, gd_heads):
    ndh = 2 * gd_heads
    half = HEAD_DIM // 2
    a = t_ref[...].T
    beta = jax.nn.sigmoid(a)
    g = -jnp.exp(alog_ref[...]) * _softplus(a + dtb_ref[...])
    ri = lax.broadcasted_iota(jnp.int32, (GD_CHUNK, GD_CHUNK), 0)
    ci = lax.broadcasted_iota(jnp.int32, (GD_CHUNK, GD_CHUNK), 1)
    g1 = g.astype(BF16)
    r1 = g - g1.astype(F32)
    g2 = r1.astype(BF16)
    g3 = (r1 - g2.astype(F32)).astype(BF16)

    def cumsum(tri):
        tb = jnp.where(tri, 1.0, 0.0).astype(BF16)
        return _dot(g1, tb) + _dot(g2, tb) + _dot(g3, tb)

    gc = jnp.where(ri < half + gd_heads, cumsum(ri <= ci), cumsum(ri >= ci))
    glast = jnp.sum(g, axis=1, keepdims=True)
    e = jnp.exp(gc)
    f = jnp.exp(glast - gc)
    gt = jnp.exp(glast) + jnp.zeros_like(gc)
    gct = gc.T
    zero_rows = jnp.zeros((3, GD_CHUNK), F32)
    for dh in range(ndh):
        r = half + dh
        rows_ref[dh, 0:1, :] = beta[dh:dh + 1]
        rows_ref[dh, 1:2, :] = beta[dh:dh + 1] * e[r:r + 1]
        rows_ref[dh, 2:3, :] = f[r:r + 1]
        rows_ref[dh, 3:4, :] = gt[r:r + 1]
        rows_ref[dh, 4:5, :] = e[r:r + 1]
        rows_ref[dh, 5:8, :] = zero_rows
        col = jnp.sum(jnp.where(ci == r, gct, 0.0), axis=1, keepdims=True)
        diff = gc[r:r + 1] - col
        mask = (ci >= ri) if dh < gd_heads else (ci <= ri)
        dt_ref[dh] = jnp.where(mask, jnp.exp(jnp.where(mask, diff, 0.0)), 0.0)


def _gdn_gates(tail, alog_t, dtb_t, gd_heads):
    m = tail.shape[0]
    nch = m // GD_CHUNK
    ndh = 2 * gd_heads
    return pl.pallas_call(
        functools.partial(_gdn_gate_kernel, gd_heads=gd_heads),
        grid=(nch,),
        in_specs=[pl.BlockSpec((GD_CHUNK, HEAD_DIM), lambda n: (n, 0)),
                  pl.BlockSpec((HEAD_DIM, GD_CHUNK), lambda n: (0, 0)),
                  pl.BlockSpec((HEAD_DIM, GD_CHUNK), lambda n: (0, 0))],
        out_specs=[pl.BlockSpec((None, ndh, 8, GD_CHUNK), lambda n: (n, 0, 0, 0)),
                   pl.BlockSpec((None, ndh, GD_CHUNK, GD_CHUNK), lambda n: (n, 0, 0, 0))],
        out_shape=[jax.ShapeDtypeStruct((nch, ndh, 8, GD_CHUNK), F32),
                   jax.ShapeDtypeStruct((nch, ndh, GD_CHUNK, GD_CHUNK), F32)],
        compiler_params=_cparams(("parallel",)),
        name="gdn_gates",
    )(tail, alog_t, dtb_t)


def _tri_inverse_multi(mnegs, ri, ci):
    eye = jnp.where(ri == ci, 1.0, 0.0)

    def blk(n):
        return (ri // n) == (ci // n)

    b16 = blk(16)
    qs = [jnp.where(b16, m, 0.0) for m in mnegs]
    xs = [eye + q for q in qs]
    qbs = [q.astype(BF16) for q in qs]
    qs = [_dot(qb, qb) for qb in qbs]
    for _ in range(2):
        qbs = [q.astype(BF16) for q in qs]
        xq = [_dot(x.astype(BF16), qb) for x, qb in zip(xs, qbs)]
        qs = [_dot(qb, qb) for qb in qbs]
        xs = [x + d for x, d in zip(xs, xq)]
    xq = [_dot(x.astype(BF16), q.astype(BF16)) for x, q in zip(xs, qs)]
    xs = [x + d for x, d in zip(xs, xq)]
    for n in (32, 64, 128):
        msk = blk(n) & jnp.logical_not(blk(n // 2))
        offs = [jnp.where(msk, m, 0.0).astype(BF16) for m in mnegs]
        xbs = [x.astype(BF16) for x in xs]
        t1 = [_dot(xb, off).astype(BF16) for xb, off in zip(xbs, offs)]
        t2 = [_dot(t, xb) for t, xb in zip(t1, xbs)]
        xs = [x + d for x, d in zip(xs, t2)]
    return xs


def _gdn_kernel(*refs, ncc, ncx, hp, unroll):
    (qx, kx, vx, qc, kc, vc, rx0, rx1, rc0, rc1, dx0, dx1, dc0, dc1) = refs[:14]
    zx = refs[14:14 + hp]
    zc = refs[14 + hp:14 + 2 * hp]
    nw, ox, oc, ut_s, wq_s, at_s, kf_s, o_s = refs[14 + 2 * hp:]
    ri = lax.broadcasted_iota(jnp.int32, (GD_CHUNK, GD_CHUNK), 0)
    ci = lax.broadcasted_iota(jnp.int32, (GD_CHUNK, GD_CHUNK), 1)
    strict = (ci > ri, ci < ri)

    def local(units, base, q_ref, k_ref, v_ref, rows, dts):
        kts = [k_ref[p, i] for p, i in units]
        kbs = [kt.astype(BF16) for kt in kts]
        kks = [_dot_tn(kb, kb) for kb in kbs]
        qks = [_dot_tn(kb, q_ref[p, i].astype(BF16)) for kb, (p, i) in zip(kbs, units)]
        chains = [(u, d) for u in range(len(units)) for d in range(2)]
        mnegs = []
        for u, d in chains:
            p, i = units[u]
            mnegs.append(-(jnp.where(strict[d], dts[d][i, p], 0.0) * kks[u] * rows[d][i, p][0:1]))
        tts = _tri_inverse_multi(mnegs, ri, ci)
        sols = []
        for (u, d), tt in zip(chains, tts):
            p, i = units[u]
            rw = rows[d][i, p]
            rhs = jnp.concatenate([v_ref[p, i] * rw[0:1], kts[u] * rw[1:2]], axis=0).astype(BF16)
            sols.append(_dot(rhs, tt.astype(BF16)))
        for (u, d), sol in zip(chains, sols):
            p, i = units[u]
            n = base + i
            rw = rows[d][i, p]
            ut_s[p, d, n] = sol[:HEAD_DIM]
            wq_s[p, d, n, :, 0:GD_CHUNK] = sol[HEAD_DIM:].astype(BF16)
            wq_s[p, d, n, :, GD_CHUNK:2 * GD_CHUNK] = (q_ref[p, i] * rw[4:5]).astype(BF16)
            at_s[p, d, n] = (qks[u] * dts[d][i, p]).astype(BF16)
            kf_s[p, d, n] = (kts[u] * rw[2:3]).astype(BF16)
        for p, i in units:
            o_s[p, base + i] = jnp.zeros((HEAD_DIM, GD_CHUNK), F32)

    def seg_local(q_ref, k_ref, v_ref, rows, dts, count, base):
        per = min(unroll, count)

        def body(it, carry):
            units = [(p, it * per + u) for u in range(per) for p in range(hp)]
            local(units, base, q_ref, k_ref, v_ref, rows, dts)
            return carry
        lax.fori_loop(0, count // per, body, 0)

    seg_local(qc, kc, vc, (rc0, rc1), (dc0, dc1), ncc, 0)
    seg_local(qx, kx, vx, (rx0, rx1), (dx0, dx1), ncx, ncc)

    def seg_scan(states, rows, count, base):
        def body(i, carry):
            j = count - 1 - i
            chains = [(p, d, (i, j)[d]) for p in range(hp) for d in range(2)]
            ys = [_dot(st.astype(BF16), wq_s[p, d, base + c]) for st, (p, d, c) in zip(carry, chains)]
            vnbs = [(ut_s[p, d, base + c] - y[:, 0:GD_CHUNK]).astype(BF16) for y, (p, d, c) in zip(ys, chains)]
            ots = [_dot(vnb, at_s[p, d, base + c]) for vnb, (p, d, c) in zip(vnbs, chains)]
            upd = [_dot_nt(vnb, kf_s[p, d, base + c]) for vnb, (p, d, c) in zip(vnbs, chains)]
            for y, ot, (p, d, c) in zip(ys, ots, chains):
                o_s[p, base + c] += y[:, GD_CHUNK:2 * GD_CHUNK] + ot
            return tuple(st * rows[d][c, p][3:4] + du for st, du, (p, d, c) in zip(carry, upd, chains))
        return lax.fori_loop(0, count, body, states)

    zero = jnp.zeros((HEAD_DIM, HEAD_DIM), F32)
    states = seg_scan((zero,) * (2 * hp), (rc0, rc1), ncc, 0)
    seg_scan(states, (rx0, rx1), ncx, ncc)

    def seg_out(o_ref, z_refs, count, base):
        def body(i, carry):
            rs = pl.ds(pl.multiple_of(i * GD_CHUNK, GD_CHUNK), GD_CHUNK)
            for p in range(hp):
                ot = o_s[p, base + i]
                ms = jnp.mean(ot * ot, axis=0, keepdims=True)
                on = (ot * lax.rsqrt(ms + LN_EPS)).T
                z = z_refs[p][rs, :].astype(F32)
                o_ref[rs, p * HEAD_DIM:(p + 1) * HEAD_DIM] = (on * nw[...] * _silu(z)).astype(o_ref.dtype)
            return carry
        lax.fori_loop(0, count, body, 0)

    seg_out(oc, zc, ncc, 0)
    seg_out(ox, zx, ncx, ncc)


def _gdn_scan(gt_x, gt_c, rows_x, rows_c, dt_x, dt_c, px, pc, norm_w, l, b, t, lc, gd_heads, z_blk, hp, unroll):
    ncx, ncc = t // GD_CHUNK, lc // GD_CHUNK
    nc = ncx + ncc
    gd_d = gd_heads * HEAD_DIM
    nhb = gd_heads // hp

    def tile_spec(n, which):
        return pl.BlockSpec((hp, n, HEAD_DIM, GD_CHUNK), lambda i, h: (which * nhb + h, i, 0, 0))

    def row_spec(n, d):
        return pl.BlockSpec((n, hp, 8, GD_CHUNK), lambda i, h: (i, d * nhb + h, 0, 0))

    def dt_spec(n, d):
        return pl.BlockSpec((n, hp, GD_CHUNK, GD_CHUNK), lambda i, h: (i, d * nhb + h, 0, 0))

    def z_spec(n, p):
        return pl.BlockSpec((n, HEAD_DIM), lambda i, h: (i, z_blk + h * hp + p))

    return pl.pallas_call(
        functools.partial(_gdn_kernel, ncc=ncc, ncx=ncx, hp=hp, unroll=unroll),
        grid=(b, nhb),
        in_specs=[tile_spec(ncx, 0), tile_spec(ncx, 1), tile_spec(ncx, 2),
                  tile_spec(ncc, 0), tile_spec(ncc, 1), tile_spec(ncc, 2),
                  row_spec(ncx, 0), row_spec(ncx, 1), row_spec(ncc, 0), row_spec(ncc, 1),
                  dt_spec(ncx, 0), dt_spec(ncx, 1), dt_spec(ncc, 0), dt_spec(ncc, 1)]
                 + [z_spec(t, p) for p in range(hp)] + [z_spec(lc, p) for p in range(hp)]
                 + [pl.BlockSpec((None, 1, HEAD_DIM), lambda i, h: (l, 0, 0))],
        out_specs=[pl.BlockSpec((t, hp * HEAD_DIM), lambda i, h: (i, h)),
                   pl.BlockSpec((lc, hp * HEAD_DIM), lambda i, h: (i, h))],
        out_shape=[jax.ShapeDtypeStruct((b * t, gd_d), BF16),
                   jax.ShapeDtypeStruct((b * lc, gd_d), BF16)],
        scratch_shapes=[pltpu.VMEM((hp, 2, nc, HEAD_DIM, GD_CHUNK), F32),
                        pltpu.VMEM((hp, 2, nc, HEAD_DIM, 2 * GD_CHUNK), BF16),
                        pltpu.VMEM((hp, 2, nc, GD_CHUNK, GD_CHUNK), BF16),
                        pltpu.VMEM((hp, 2, nc, HEAD_DIM, GD_CHUNK), BF16),
                        pltpu.VMEM((hp, nc, HEAD_DIM, GD_CHUNK), F32)],
        compiler_params=_cparams(("parallel", "arbitrary")),
        name="gdn_scan",
    )(gt_x, gt_x, gt_x, gt_c, gt_c, gt_c, rows_x, rows_x, rows_c, rows_c, dt_x, dt_x, dt_c, dt_c,
      *([px] * hp), *([pc] * hp), norm_w)


def _outproj_kernel(x_ref, na_ref, cv_ref, gd_ref, w_ref, gt_ref, lg_ref, lb_ref, o_ref, *, alpha):
    n0 = na_ref.shape[1]
    n1 = n0 + cv_ref.shape[1]
    y = (_dot(na_ref[...], w_ref[0:n0, :]) + _dot(cv_ref[...], w_ref[n0:n1, :])
         + _dot(gd_ref[...], w_ref[n1:, :]))
    z = alpha * x_ref[...] + gt_ref[...] * y
    o_ref[...] = _ln(z) * lg_ref[...] + lb_ref[...]


def _outproj(x, o_na, o_cv, o_gd, w_out, ada, ln_g, ln_b, l, grp, tm, alpha):
    m, d = x.shape
    d_mix = w_out.shape[1]
    return pl.pallas_call(
        functools.partial(_outproj_kernel, alpha=alpha),
        grid=(m // tm,),
        in_specs=[pl.BlockSpec((tm, d), lambda i: (i, 0)),
                  pl.BlockSpec((tm, o_na.shape[1]), lambda i: (i, 0)),
                  pl.BlockSpec((tm, o_cv.shape[1]), lambda i: (i, 0)),
                  pl.BlockSpec((tm, o_gd.shape[1]), lambda i: (i, 0)),
                  pl.BlockSpec((None, d_mix, d), lambda i: (l, 0, 0)),
                  _ada_spec(d, l, grp, 5), _ln_spec(d, l, 1), _ln_spec(d, l, 1)],
        out_specs=pl.BlockSpec((tm, d), lambda i: (i, 0)),
        out_shape=jax.ShapeDtypeStruct((m, d), F32),
        compiler_params=_cparams(("parallel",)),
        name="outproj",
    )(x, o_na, o_cv, o_gd, w_out, ada, ln_g, ln_b)


def _rope_tables(t):
    tok = jnp.arange(t, dtype=jnp.int32)
    row = (tok // GRID_W).astype(F32)
    col = (tok % GRID_W).astype(F32)
    n_freq = HEAD_DIM // 4
    inv_freq = ROPE_BASE ** (-jnp.arange(n_freq, dtype=F32) / n_freq)
    ar = (row[:, None] * inv_freq).T
    ac = (col[:, None] * inv_freq).T
    cos_t = jnp.concatenate([jnp.cos(ar), jnp.cos(ar), jnp.cos(ac), jnp.cos(ac)], axis=0)
    sin_t = jnp.concatenate([-jnp.sin(ar), jnp.sin(ar), -jnp.sin(ac), jnp.sin(ac)], axis=0)
    return cos_t, sin_t


def kernel(x, c, ctx, c_ctx, w_ada, b_ada, ln_g, ln_b, ffn1_w_gu, ffn1_w_down, w_in, na_rpb, cv_conv_w,
           gd_conv_w, gd_a_log, gd_dt_bias, gd_norm_w, w_out, ffn2_w_gu, ffn2_w_down):
    b, t, d = x.shape
    lc = ctx.shape[1]
    depth = w_ada.shape[0]
    d_ff = ffn1_w_down.shape[1]
    na_heads = na_rpb.shape[1]
    win_r, win_c = (na_rpb.shape[2] + 1) // 2, (na_rpb.shape[3] + 1) // 2
    cv_d = cv_conv_w.shape[2]
    cv_groups = cv_d // HEAD_DIM
    gd_heads = gd_a_log.shape[2]
    na_d, gd_d = na_heads * HEAD_DIM, gd_heads * HEAD_DIM
    ndh = 2 * gd_heads
    n_main = 3 * na_d + 3 * cv_d + 3 * gd_d + gd_d
    alpha = (2 * depth) ** 0.25
    assert b + 1 <= 8 and t % GD_CHUNK == 0 and lc % GD_CHUNK == 0 and ndh <= HEAD_DIM // 2
    assert t // GRID_W >= win_r

    tm = math.gcd(math.gcd(t, b * lc), 512)
    tf = 512 if d_ff % 512 == 0 else 128
    tn_ada = 1024 if (N_ADA * d) % 1024 == 0 else 128
    tm_in = math.gcd(math.gcd(t, b * lc), 1024)
    tn_in = math.gcd(n_main, 768)
    gd_hp = 2 if gd_heads % 2 == 0 else 1

    bf = lambda w: w.astype(BF16)
    w_gu1, w_dn1, w_gu2, w_dn2, w_out_b = map(bf, (ffn1_w_gu, ffn1_w_down, ffn2_w_gu, ffn2_w_down, w_out))
    half = HEAD_DIM // 2
    w_tail = jnp.zeros((depth, d, HEAD_DIM), BF16)
    w_tail = w_tail.at[:, :, 0:ndh].set(bf(w_in[:, :, n_main:n_main + ndh]))
    w_tail = w_tail.at[:, :, half:half + ndh].set(bf(w_in[:, :, n_main + ndh:n_main + 2 * ndh]))
    alog_t = jnp.zeros((depth, HEAD_DIM, GD_CHUNK), F32).at[:, half:half + ndh, :].set(
        jnp.broadcast_to(gd_a_log.reshape(depth, ndh, 1), (depth, ndh, GD_CHUNK)))
    dtb_t = jnp.zeros((depth, HEAD_DIM, GD_CHUNK), F32).at[:, half:half + ndh, :].set(
        jnp.broadcast_to(gd_dt_bias.reshape(depth, ndh, 1), (depth, ndh, GD_CHUNK)))
    ln_g4 = ln_g.reshape(depth, 3, 1, d)
    ln_b4 = ln_b.reshape(depth, 3, 1, d)
    norm_w3 = gd_norm_w.reshape(depth, 1, HEAD_DIM)
    cos_x, sin_x = _rope_tables(t)
    cos_c, sin_c = jnp.ones((HEAD_DIM, lc), F32), jnp.zeros((HEAD_DIM, lc), F32)

    cvec = jnp.zeros((8, d), F32).at[0:b].set(c).at[b].set(c_ctx)
    ada = _ada_table(cvec, w_ada, b_ada, tn_ada).reshape(depth, 8, N_ADA, 1, d)

    grp_x = lambda i: (i * tm) // t
    grp_x_in = lambda i: (i * tm_in) // t
    grp_c = lambda i: b

    xs = x.reshape(b * t, d)
    cs = ctx.reshape(b * lc, d)
    cv_blk = 3 * na_heads
    gd_blk = cv_blk + 3 * cv_groups
    z_blk = gd_blk + 3 * gd_heads
    bias = _na_bias_table(na_rpb, t // GRID_W, win_r, win_c)

    for l in range(depth):
        last = l == depth - 1
        xs = _ffn(xs, ada, ln_g4, ln_b4, l, 0, 0, grp_x, w_gu1, w_dn1, tm, tf, alpha)
        cs = _ffn(cs, ada, ln_g4, ln_b4, l, 0, 0, grp_c, w_gu1, w_dn1, tm, tf, alpha)

        p_x, tl_x = _inproj(xs, ada, l, grp_x_in, w_in, w_tail, n_main, tn_in, tm_in)
        p_c, tl_c = _inproj(cs, ada, l, grp_c, w_in, w_tail, n_main, tn_in, tm_in)

        o_na_x = _na_attention(p_x, p_c, bias, l, b, t, lc, na_heads, win_r)
        o_cv_x = _short_conv(p_x, cv_conv_w, l, b, t, cv_groups, cv_blk)

        rows_x, dt_x = _gdn_gates(tl_x, alog_t[l], dtb_t[l], gd_heads)
        rows_c, dt_c = _gdn_gates(tl_c, alog_t[l], dtb_t[l], gd_heads)
        gt_x = _gdn_prep(p_x, gd_conv_w, cos_x, sin_x, l, b, t, gd_blk, gd_heads)
        gt_c = _gdn_prep(p_c, gd_conv_w, cos_c, sin_c, l, b, lc, gd_blk, gd_heads)
        o_gd_x, o_gd_c = _gdn_scan(gt_x, gt_c, rows_x, rows_c, dt_x, dt_c, p_x, p_c, norm_w3, l,
                                   b, t, lc, gd_heads, z_blk, gd_hp, 2)

        xs = _outproj(xs, o_na_x, o_cv_x, o_gd_x, w_out_b, ada, ln_g4, ln_b4, l, grp_x, tm, alpha)
        xs = _ffn(xs, ada, ln_g4, ln_b4, l, 6, 2, grp_x, w_gu2, w_dn2, tm, tf, alpha)
        if not last:
            o_na_c = _dense_attention(p_c, b, lc, na_heads)
            o_cv_c = _short_conv(p_c, cv_conv_w, l, b, lc, cv_groups, cv_blk)
            cs = _outproj(cs, o_na_c, o_cv_c, o_gd_c, w_out_b, ada, ln_g4, ln_b4, l, grp_c, tm, alpha)
            cs = _ffn(cs, ada, ln_g4, ln_b4, l, 6, 2, grp_c, w_gu2, w_dn2, tm, tf, alpha)
    return xs.reshape(b, t, d)
```

```python
import functools
import math

import jax
import jax.numpy as jnp
from jax import lax
from jax.experimental import pallas as pl
from jax.experimental.pallas import tpu as pltpu

F32 = jnp.float32
BF16 = jnp.bfloat16

HEAD_DIM = 128
GRID_W = 64
GD_CHUNK = 128
ROPE_BASE = 10000.0
LN_EPS = 1e-6
NEG_INF = -1e30
N_ADA = 9
V7X_VMEM_LIMIT = 56 * 1024 * 1024


def _cparams(sem):
    return pltpu.CompilerParams(dimension_semantics=sem, vmem_limit_bytes=V7X_VMEM_LIMIT)


def _ln(x):
    mu = jnp.mean(x, axis=-1, keepdims=True)
    xc = x - mu
    var = jnp.mean(xc * xc, axis=-1, keepdims=True)
    return xc * lax.rsqrt(var + LN_EPS)


def _silu(x):
    return x * jax.nn.sigmoid(x)


def _softplus(x):
    return jnp.maximum(x, 0.0) + jnp.log(1.0 + jnp.exp(-jnp.abs(x)))


def _dot(a, b):
    return jnp.dot(a, b, preferred_element_type=F32)


def _dot_nt(a, b):
    return lax.dot_general(a, b, (((1,), (1,)), ((), ())), preferred_element_type=F32)


def _dot_tn(a, b):
    return lax.dot_general(a, b, (((0,), (0,)), ((), ())), preferred_element_type=F32)


def _ada_kernel(c_ref, w_ref, b_ref, o_ref):
    s = _silu(c_ref[...]).astype(BF16)
    o_ref[...] = _dot(s, w_ref[...].astype(BF16)) + b_ref[...]


def _ada_table(cvec, w_ada, b_ada, tn):
    depth, d, n = w_ada.shape
    return pl.pallas_call(
        _ada_kernel,
        grid=(depth, n // tn),
        in_specs=[pl.BlockSpec((8, d), lambda l, j: (0, 0)),
                  pl.BlockSpec((None, d, tn), lambda l, j: (l, 0, j)),
                  pl.BlockSpec((None, 1, tn), lambda l, j: (l, 0, j))],
        out_specs=pl.BlockSpec((None, 8, tn), lambda l, j: (l, 0, j)),
        out_shape=jax.ShapeDtypeStruct((depth, 8, n), F32),
        compiler_params=_cparams(("parallel", "parallel")),
        name="ada_table",
    )(cvec, w_ada, b_ada.reshape(depth, 1, n))


def _ada_spec(d, l, grp, j):
    return pl.BlockSpec((None, None, None, 1, d), lambda i, *_: (l, grp(i), j, 0, 0))


def _ln_spec(d, l, k):
    return pl.BlockSpec((None, None, 1, d), lambda *_: (l, k, 0, 0))


def _ffn_kernel(x_ref, sh_ref, sc_ref, gt_ref, wg_ref, wu_ref, wd_ref, lg_ref, lb_ref, o_ref, h_ref, *, alpha):
    f = pl.program_id(1)

    @pl.when(f == 0)
    def _():
        h = _ln(x_ref[...]) * (1.0 + sc_ref[...]) + sh_ref[...]
        h_ref[...] = h.astype(BF16)
        o_ref[...] = jnp.zeros_like(o_ref)

    h = h_ref[...]
    g = _dot(h, wg_ref[...].astype(BF16))
    u = _dot(h, wu_ref[...].astype(BF16))
    a = (_silu(g) * u).astype(BF16)
    o_ref[...] += _dot(a, wd_ref[...].astype(BF16))

    @pl.when(f == pl.num_programs(1) - 1)
    def _():
        z = alpha * x_ref[...] + (0.5 * gt_ref[...]) * o_ref[...]
        o_ref[...] = _ln(z) * lg_ref[...] + lb_ref[...]


def _ffn(x, ada, ln_g, ln_b, l, j0, k_ln, grp, w_gu, w_down, tm, tf, alpha):
    m, d = x.shape
    d_ff = w_down.shape[1]
    nf = d_ff // tf
    return pl.pallas_call(
        functools.partial(_ffn_kernel, alpha=alpha),
        grid=(m // tm, nf),
        in_specs=[pl.BlockSpec((tm, d), lambda i, f: (i, 0), pipeline_mode=pl.Buffered(1)),
                  _ada_spec(d, l, grp, j0), _ada_spec(d, l, grp, j0 + 1), _ada_spec(d, l, grp, j0 + 2),
                  pl.BlockSpec((None, d, tf), lambda i, f: (l, 0, f)),
                  pl.BlockSpec((None, d, tf), lambda i, f: (l, 0, f + nf)),
                  pl.BlockSpec((None, tf, d), lambda i, f: (l, f, 0)),
                  _ln_spec(d, l, k_ln), _ln_spec(d, l, k_ln)],
        out_specs=pl.BlockSpec((tm, d), lambda i, f: (i, 0)),
        out_shape=jax.ShapeDtypeStruct((m, d), F32),
        scratch_shapes=[pltpu.VMEM((tm, d), BF16)],
        compiler_params=_cparams(("parallel", "arbitrary")),
        name="ffn",
    )(x, ada, ada, ada, w_gu, w_gu, w_down, ln_g, ln_b)


def _inproj_kernel(x_ref, sh_ref, sc_ref, w_ref, wt_ref, o_ref, t_ref, h_ref):
    @pl.when(pl.program_id(1) == 0)
    def _():
        h = _ln(x_ref[...]) * (1.0 + sc_ref[...]) + sh_ref[...]
        h_ref[...] = h.astype(BF16)
        t_ref[...] = _dot_nt(h_ref[...], wt_ref[...].astype(BF16))

    o_ref[...] = _dot_nt(h_ref[...], w_ref[...].astype(BF16)).astype(o_ref.dtype)


def _inproj(x, ada, l, grp, w_in_t, w_tail_t, n_main, tn, tm):
    m, d = x.shape
    return pl.pallas_call(
        _inproj_kernel,
        grid=(m // tm, n_main // tn),
        in_specs=[pl.BlockSpec((tm, d), lambda i, n: (i, 0)),
                  _ada_spec(d, l, grp, 3), _ada_spec(d, l, grp, 4),
                  pl.BlockSpec((None, tn, d), lambda i, n: (l, n, 0)),
                  pl.BlockSpec((None, HEAD_DIM, d), lambda i, n: (l, 0, 0))],
        out_specs=[pl.BlockSpec((tm, tn), lambda i, n: (i, n)),
                   pl.BlockSpec((tm, HEAD_DIM), lambda i, n: (i, 0))],
        out_shape=[jax.ShapeDtypeStruct((m, n_main), BF16),
                   jax.ShapeDtypeStruct((m, HEAD_DIM), F32)],
        scratch_shapes=[pltpu.VMEM((tm, d), BF16)],
        compiler_params=_cparams(("parallel", "arbitrary")),
        name="inproj",
    )(x, ada, ada, w_in_t, w_tail_t)


LOG2_E = 1.4426950408889634


def _softmax_pv(s_parts, v_parts, scale):
    m = functools.reduce(jnp.maximum, [jnp.max(s, axis=-1, keepdims=True) for s in s_parts])
    ps = [jnp.exp2((s - m) * (scale * LOG2_E)) for s in s_parts]
    den = functools.reduce(lambda a, b: a + b, [jnp.sum(p, axis=-1, keepdims=True) for p in ps])
    o = functools.reduce(lambda a, b: a + b, [_dot(p.astype(BF16), v) for p, v in zip(ps, v_parts)])
    return o / den


def _na_plan(rows, wr, win_r, rq):
    band = lambda r: min(max(r - wr // 2, 0), rows - wr)
    kinds, starts = [], []
    for rb in range(rows // rq):
        r0 = rb * rq
        u0 = min(band(r0), rows - (wr + rq))
        kind = tuple((band(r0 + i) - (r0 + i) + win_r - 1, band(r0 + i) - u0) for i in range(rq))
        if not kinds or kinds[-1] != kind:
            kinds.append(kind)
            starts.append(rb)
    return kinds, starts


def _na_kernel(q_ref, k_ref, v_ref, kc_ref, vc_ref, b_ref, o_ref, *, heads, rows, wr, rq, scale):
    r0 = pl.program_id(1) * rq
    nbr = wr + rq
    u0 = jnp.minimum(jnp.clip(r0 - wr // 2, 0, rows - wr), rows - nbr)
    start = pl.multiple_of(u0 * GRID_W, GRID_W)
    nb = nbr * GRID_W

    def scores(h):
        cs = slice(h * HEAD_DIM, (h + 1) * HEAD_DIM)
        q = q_ref[:, cs]
        return [_dot_nt(q, k_ref[pl.ds(start, nb), cs]) + b_ref[h], _dot_nt(q, kc_ref[:, cs])]

    pending = scores(0)
    for h in range(heads):
        nxt = scores(h + 1) if h + 1 < heads else None
        cs = slice(h * HEAD_DIM, (h + 1) * HEAD_DIM)
        o = _softmax_pv(pending, [v_ref[pl.ds(start, nb), cs], vc_ref[:, cs]], scale)
        o_ref[:, cs] = o.astype(o_ref.dtype)
        pending = nxt


def _na_attention(px, pc, bias, starts, l, b, t, lc, heads, win_r, rq):
    na_d = heads * HEAD_DIM
    rows = t // GRID_W
    wr = min(win_r, rows)
    nrb = rows // rq
    nb = (wr + rq) * GRID_W

    def kind(r):
        return sum((r >= s0).astype(jnp.int32) for s0 in starts[1:]) if len(starts) > 1 else 0

    return pl.pallas_call(
        functools.partial(_na_kernel, heads=heads, rows=rows, wr=wr, rq=rq, scale=HEAD_DIM ** -0.5),
        grid=(b, nrb),
        in_specs=[pl.BlockSpec((rq * GRID_W, na_d), lambda i, r: (i * nrb + r, 0)),
                  pl.BlockSpec((t, na_d), lambda i, r: (i, 1)),
                  pl.BlockSpec((t, na_d), lambda i, r: (i, 2)),
                  pl.BlockSpec((lc, na_d), lambda i, r: (i, 1)),
                  pl.BlockSpec((lc, na_d), lambda i, r: (i, 2)),
                  pl.BlockSpec((None, heads, None, rq * GRID_W, nb), lambda i, r: (l, 0, kind(r), 0, 0))],
        out_specs=pl.BlockSpec((rq * GRID_W, na_d), lambda i, r: (i * nrb + r, 0)),
        out_shape=jax.ShapeDtypeStruct((b * t, na_d), BF16),
        compiler_params=_cparams(("parallel", "arbitrary")),
        name="na_attention",
    )(px, px, px, pc, pc, bias)


def _dense_attn_kernel(q_ref, k_ref, v_ref, o_ref, *, heads, scale):
    for h in range(heads):
        cs = slice(h * HEAD_DIM, (h + 1) * HEAD_DIM)
        s = _dot_nt(q_ref[:, cs], k_ref[:, cs])
        o_ref[:, cs] = _softmax_pv([s], [v_ref[:, cs]], scale).astype(o_ref.dtype)


def _dense_attention(pc, b, lc, heads):
    na_d = heads * HEAD_DIM
    return pl.pallas_call(
        functools.partial(_dense_attn_kernel, heads=heads, scale=HEAD_DIM ** -0.5),
        grid=(b,),
        in_specs=[pl.BlockSpec((lc, na_d), lambda i: (i, 0)),
                  pl.BlockSpec((lc, na_d), lambda i: (i, 1)),
                  pl.BlockSpec((lc, na_d), lambda i: (i, 2))],
        out_specs=pl.BlockSpec((lc, na_d), lambda i: (i, 0)),
        out_shape=jax.ShapeDtypeStruct((b * lc, na_d), BF16),
        compiler_params=_cparams(("parallel",)),
        name="ctx_attention",
    )(pc, pc, pc)


def _na_bias_kernel(r_ref, o_ref, *, win_r, win_c, wr, kinds, inv_scale):
    n_dr, n_dc = 2 * win_r - 1, 2 * win_c - 1
    base = (pl.program_id(0) * pl.num_programs(1) + pl.program_id(1)) * (n_dr * n_dc)
    q = lax.broadcasted_iota(jnp.int32, (GRID_W, GRID_W), 0)
    k = lax.broadcasted_iota(jnp.int32, (GRID_W, GRID_W), 1)
    dc = jnp.clip(k - q + win_c - 1, 0, n_dc - 1)
    col_start = jnp.clip(q - win_c // 2, 0, GRID_W - win_c)
    col_ok = (k >= col_start) & (k < col_start + win_c)
    o_ref[...] = jnp.full(o_ref.shape, NEG_INF, F32)
    for a in range(n_dr):
        acc = jnp.zeros((GRID_W, GRID_W), F32)
        for j in range(n_dc):
            acc = jnp.where(dc == j, r_ref[base + a * n_dc + j], acc)
        g = jnp.where(col_ok, acc * inv_scale, NEG_INF)
        for kd, kind in enumerate(kinds):
            for i, (dr0, off) in enumerate(kind):
                w = a - dr0
                if 0 <= w < wr:
                    o_ref[kd, i * GRID_W:(i + 1) * GRID_W, (off + w) * GRID_W:(off + w + 1) * GRID_W] = g


def _na_bias_table(rpb, kinds, rows, win_r, win_c, rq):
    depth, heads = rpb.shape[0], rpb.shape[1]
    wr = min(win_r, rows)
    shape = (len(kinds), rq * GRID_W, (wr + rq) * GRID_W)
    return pl.pallas_call(
        functools.partial(_na_bias_kernel, win_r=win_r, win_c=win_c, wr=wr, kinds=kinds, inv_scale=HEAD_DIM ** 0.5),
        grid=(depth, heads),
        in_specs=[pl.BlockSpec(memory_space=pltpu.SMEM)],
        out_specs=pl.BlockSpec((None, None) + shape, lambda l, h: (l, h, 0, 0, 0)),
        out_shape=jax.ShapeDtypeStruct((depth, heads) + shape, F32),
        compiler_params=_cparams(("parallel", "parallel")),
        name="na_bias",
    )(rpb.reshape(-1))


SUBLANES = 8


def _dwconv_rows(x, w, pad_ref):
    ksz, t = w.shape[0], x.shape[0]
    assert ksz // 2 <= SUBLANES
    halo = jnp.zeros((SUBLANES, x.shape[1]), F32)
    pad_ref[0:SUBLANES, :] = halo
    pad_ref[SUBLANES + t:2 * SUBLANES + t, :] = halo
    pad_ref[SUBLANES:SUBLANES + t, :] = x
    y = None
    for j in range(ksz):
        start = SUBLANES + j - ksz // 2
        term = w[j:j + 1, :] * pad_ref[start:start + t, :]
        y = term if y is None else y + term
    return y


def _cv_kernel(b_ref, c_ref, u_ref, w_ref, o_ref, pad_ref):
    cu = c_ref[...].astype(F32) * u_ref[...].astype(F32)
    o_ref[...] = (b_ref[...].astype(F32) * _dwconv_rows(cu, w_ref[...], pad_ref)).astype(o_ref.dtype)


def _short_conv(p, cv_w, l, nseq, tseq, cv_groups, col0_blk):
    ksz = cv_w.shape[1]
    cv_d = cv_groups * HEAD_DIM
    return pl.pallas_call(
        _cv_kernel,
        grid=(nseq, cv_groups),
        in_specs=[pl.BlockSpec((tseq, HEAD_DIM), lambda s, j: (s, col0_blk + j)),
                  pl.BlockSpec((tseq, HEAD_DIM), lambda s, j: (s, col0_blk + cv_groups + j)),
                  pl.BlockSpec((tseq, HEAD_DIM), lambda s, j: (s, col0_blk + 2 * cv_groups + j)),
                  pl.BlockSpec((None, ksz, HEAD_DIM), lambda s, j: (l, 0, j))],
        out_specs=pl.BlockSpec((tseq, HEAD_DIM), lambda s, j: (s, j)),
        out_shape=jax.ShapeDtypeStruct((nseq * tseq, cv_d), BF16),
        scratch_shapes=[pltpu.VMEM((tseq + 2 * SUBLANES, HEAD_DIM), F32)],
        compiler_params=_cparams(("parallel", "parallel")),
        name="short_conv",
    )(p, p, p, cv_w)


def _gdn_prep_kernel(x_ref, w_ref, cos_ref, sin_ref, o_ref, pad_ref, *, gd_heads):
    j = pl.program_id(1)
    y = _dwconv_rows(x_ref[...].astype(F32), w_ref[...], pad_ref)
    st = _silu(y).T
    ss = jnp.sum(st * st, axis=0, keepdims=True)
    nrm = st * lax.rsqrt(ss + LN_EPS)
    q4 = HEAD_DIM // 4
    swapped = jnp.concatenate([nrm[q4:2 * q4], nrm[0:q4], nrm[3 * q4:4 * q4], nrm[2 * q4:3 * q4]], axis=0)
    rot = nrm * cos_ref[...] + swapped * sin_ref[...]
    rot = rot * jnp.where(j < gd_heads, HEAD_DIM ** -0.5, 1.0)
    res = jnp.where(j < 2 * gd_heads, rot, st)
    for c in range(o_ref.shape[0]):
        o_ref[c] = res[:, c * GD_CHUNK:(c + 1) * GD_CHUNK]


def _gdn_prep(p, conv_w, cos_t, sin_t, l, nseq, tseq, col0_blk, gd_heads):
    ksz = conv_w.shape[1]
    nch = tseq // GD_CHUNK
    return pl.pallas_call(
        functools.partial(_gdn_prep_kernel, gd_heads=gd_heads),
        grid=(nseq, 3 * gd_heads),
        in_specs=[pl.BlockSpec((tseq, HEAD_DIM), lambda s, j: (s, col0_blk + j)),
                  pl.BlockSpec((None, ksz, HEAD_DIM), lambda s, j: (l, 0, j)),
                  pl.BlockSpec((HEAD_DIM, tseq), lambda s, j: (0, 0)),
                  pl.BlockSpec((HEAD_DIM, tseq), lambda s, j: (0, 0))],
        out_specs=pl.BlockSpec((None, nch, HEAD_DIM, GD_CHUNK), lambda s, j: (j, s, 0, 0)),
        out_shape=jax.ShapeDtypeStruct((3 * gd_heads, nseq * nch, HEAD_DIM, GD_CHUNK), F32),
        scratch_shapes=[pltpu.VMEM((tseq + 2 * SUBLANES, HEAD_DIM), F32)],
        compiler_params=_cparams(("parallel", "parallel")),
        name="gdn_prep",
    )(p, conv_w, cos_t, sin_t)


def _gdn_gate_kernel(t_ref, alog_ref, dtb_ref, rows_ref, dt_ref, *, gd_heads):
    ndh = 2 * gd_heads
    half = HEAD_DIM // 2
    a = t_ref[...].T
    beta = jax.nn.sigmoid(a)
    g = -jnp.exp(alog_ref[...]) * _softplus(a + dtb_ref[...])
    ri = lax.broadcasted_iota(jnp.int32, (GD_CHUNK, GD_CHUNK), 0)
    ci = lax.broadcasted_iota(jnp.int32, (GD_CHUNK, GD_CHUNK), 1)
    g1 = g.astype(BF16)
    r1 = g - g1.astype(F32)
    g2 = r1.astype(BF16)
    g3 = (r1 - g2.astype(F32)).astype(BF16)

    def cumsum(tri):
        tb = jnp.where(tri, 1.0, 0.0).astype(BF16)
        return _dot(g1, tb) + _dot(g2, tb) + _dot(g3, tb)

    gc = jnp.where(ri < half + gd_heads, cumsum(ri <= ci), cumsum(ri >= ci))
    glast = jnp.sum(g, axis=1, keepdims=True)
    e = jnp.exp(gc)
    f = jnp.exp(glast - gc)
    gt = jnp.exp(glast) + jnp.zeros_like(gc)
    gct = gc.T
    zero_rows = jnp.zeros((3, GD_CHUNK), F32)
    for dh in range(ndh):
        r = half + dh
        rows_ref[dh, 0:1, :] = beta[dh:dh + 1]
        rows_ref[dh, 1:2, :] = beta[dh:dh + 1] * e[r:r + 1]
        rows_ref[dh, 2:3, :] = f[r:r + 1]
        rows_ref[dh, 3:4, :] = gt[r:r + 1]
        rows_ref[dh, 4:5, :] = e[r:r + 1]
        rows_ref[dh, 5:8, :] = zero_rows
        col = jnp.sum(jnp.where(ci == r, gct, 0.0), axis=1, keepdims=True)
        diff = gc[r:r + 1] - col
        mask = (ci >= ri) if dh < gd_heads else (ci <= ri)
        dt_ref[dh] = jnp.where(mask, jnp.exp(jnp.where(mask, diff, 0.0)), 0.0)


def _gdn_gates(tail, alog_t, dtb_t, gd_heads):
    m = tail.shape[0]
    nch = m // GD_CHUNK
    ndh = 2 * gd_heads
    return pl.pallas_call(
        functools.partial(_gdn_gate_kernel, gd_heads=gd_heads),
        grid=(nch,),
        in_specs=[pl.BlockSpec((GD_CHUNK, HEAD_DIM), lambda n: (n, 0)),
                  pl.BlockSpec((HEAD_DIM, GD_CHUNK), lambda n: (0, 0)),
                  pl.BlockSpec((HEAD_DIM, GD_CHUNK), lambda n: (0, 0))],
        out_specs=[pl.BlockSpec((None, ndh, 8, GD_CHUNK), lambda n: (n, 0, 0, 0)),
                   pl.BlockSpec((None, ndh, GD_CHUNK, GD_CHUNK), lambda n: (n, 0, 0, 0))],
        out_shape=[jax.ShapeDtypeStruct((nch, ndh, 8, GD_CHUNK), F32),
                   jax.ShapeDtypeStruct((nch, ndh, GD_CHUNK, GD_CHUNK), F32)],
        compiler_params=_cparams(("parallel",)),
        name="gdn_gates",
    )(tail, alog_t, dtb_t)


def _tri_inverse_multi(mnegs, ri, ci):
    eye = jnp.where(ri == ci, 1.0, 0.0)

    def blk(n):
        return (ri // n) == (ci // n)

    b16 = blk(16)
    qs = [jnp.where(b16, m, 0.0) for m in mnegs]
    xs = [eye + q for q in qs]
    qbs = [q.astype(BF16) for q in qs]
    qs = [_dot(qb, qb) for qb in qbs]
    for _ in range(2):
        qbs = [q.astype(BF16) for q in qs]
        xq = [_dot(x.astype(BF16), qb) for x, qb in zip(xs, qbs)]
        qs = [_dot(qb, qb) for qb in qbs]
        xs = [x + d for x, d in zip(xs, xq)]
    xq = [_dot(x.astype(BF16), q.astype(BF16)) for x, q in zip(xs, qs)]
    xs = [x + d for x, d in zip(xs, xq)]
    for n in (32, 64, 128):
        msk = blk(n) & jnp.logical_not(blk(n // 2))
        offs = [jnp.where(msk, m, 0.0).astype(BF16) for m in mnegs]
        xbs = [x.astype(BF16) for x in xs]
        t1 = [_dot(xb, off).astype(BF16) for xb, off in zip(xbs, offs)]
        t2 = [_dot(t, xb) for t, xb in zip(t1, xbs)]
        xs = [x + d for x, d in zip(xs, t2)]
    return xs


def _gdn_kernel(*refs, ncc, ncx, hp, unroll):
    (qx, kx, vx, qc, kc, vc, rx0, rx1, rc0, rc1, dx0, dx1, dc0, dc1) = refs[:14]
    zx = refs[14:14 + hp]
    zc = refs[14 + hp:14 + 2 * hp]
    nw, ox, oc, ut_s, wq_s, at_s, kf_s, o_s = refs[14 + 2 * hp:]
    ri = lax.broadcasted_iota(jnp.int32, (GD_CHUNK, GD_CHUNK), 0)
    ci = lax.broadcasted_iota(jnp.int32, (GD_CHUNK, GD_CHUNK), 1)
    strict = (ci > ri, ci < ri)

    def local(units, base, q_ref, k_ref, v_ref, rows, dts):
        kts = [k_ref[p, i] for p, i in units]
        kbs = [kt.astype(BF16) for kt in kts]
        kks = [_dot_tn(kb, kb) for kb in kbs]
        qks = [_dot_tn(kb, q_ref[p, i].astype(BF16)) for kb, (p, i) in zip(kbs, units)]
        chains = [(u, d) for u in range(len(units)) for d in range(2)]
        mnegs = []
        for u, d in chains:
            p, i = units[u]
            mnegs.append(-(jnp.where(strict[d], dts[d][i, p], 0.0) * kks[u] * rows[d][i, p][0:1]))
        tts = _tri_inverse_multi(mnegs, ri, ci)
        sols = []
        for (u, d), tt in zip(chains, tts):
            p, i = units[u]
            rw = rows[d][i, p]
            rhs = jnp.concatenate([v_ref[p, i] * rw[0:1], kts[u] * rw[1:2]], axis=0).astype(BF16)
            sols.append(_dot(rhs, tt.astype(BF16)))
        for (u, d), sol in zip(chains, sols):
            p, i = units[u]
            n = base + i
            rw = rows[d][i, p]
            ut_s[p, d, n] = sol[:HEAD_DIM]
            wq_s[p, d, n, :, 0:GD_CHUNK] = sol[HEAD_DIM:].astype(BF16)
            wq_s[p, d, n, :, GD_CHUNK:2 * GD_CHUNK] = (q_ref[p, i] * rw[4:5]).astype(BF16)
            at_s[p, d, n] = (qks[u] * dts[d][i, p]).astype(BF16)
            kf_s[p, d, n] = (kts[u] * rw[2:3]).astype(BF16)
        for p, i in units:
            o_s[p, base + i] = jnp.zeros((HEAD_DIM, GD_CHUNK), F32)

    def seg_local(q_ref, k_ref, v_ref, rows, dts, count, base):
        per = min(unroll, count)

        def body(it, carry):
            units = [(p, it * per + u) for u in range(per) for p in range(hp)]
            local(units, base, q_ref, k_ref, v_ref, rows, dts)
            return carry
        lax.fori_loop(0, count // per, body, 0)

    seg_local(qc, kc, vc, (rc0, rc1), (dc0, dc1), ncc, 0)
    seg_local(qx, kx, vx, (rx0, rx1), (dx0, dx1), ncx, ncc)

    def seg_scan(states, rows, count, base):
        def body(i, carry):
            j = count - 1 - i
            chains = [(p, d, (i, j)[d]) for p in range(hp) for d in range(2)]
            ys = [_dot(st.astype(BF16), wq_s[p, d, base + c]) for st, (p, d, c) in zip(carry, chains)]
            vnbs = [(ut_s[p, d, base + c] - y[:, 0:GD_CHUNK]).astype(BF16) for y, (p, d, c) in zip(ys, chains)]
            ots = [_dot(vnb, at_s[p, d, base + c]) for vnb, (p, d, c) in zip(vnbs, chains)]
            upd = [_dot_nt(vnb, kf_s[p, d, base + c]) for vnb, (p, d, c) in zip(vnbs, chains)]
            for y, ot, (p, d, c) in zip(ys, ots, chains):
                o_s[p, base + c] += y[:, GD_CHUNK:2 * GD_CHUNK] + ot
            return tuple(st * rows[d][c, p][3:4] + du for st, du, (p, d, c) in zip(carry, upd, chains))
        return lax.fori_loop(0, count, body, states)

    zero = jnp.zeros((HEAD_DIM, HEAD_DIM), F32)
    states = seg_scan((zero,) * (2 * hp), (rc0, rc1), ncc, 0)
    seg_scan(states, (rx0, rx1), ncx, ncc)

    def seg_out(o_ref, z_refs, count, base):
        def body(i, carry):
            rs = pl.ds(pl.multiple_of(i * GD_CHUNK, GD_CHUNK), GD_CHUNK)
            for p in range(hp):
                ot = o_s[p, base + i]
                ms = jnp.mean(ot * ot, axis=0, keepdims=True)
                on = (ot * lax.rsqrt(ms + LN_EPS)).T
                z = z_refs[p][rs, :].astype(F32)
                o_ref[rs, p * HEAD_DIM:(p + 1) * HEAD_DIM] = (on * nw[...] * _silu(z)).astype(o_ref.dtype)
            return carry
        lax.fori_loop(0, count, body, 0)

    seg_out(oc, zc, ncc, 0)
    seg_out(ox, zx, ncx, ncc)


def _gdn_scan(gt_x, gt_c, rows_x, rows_c, dt_x, dt_c, px, pc, norm_w, l, b, t, lc, gd_heads, z_blk, hp, unroll):
    ncx, ncc = t // GD_CHUNK, lc // GD_CHUNK
    nc = ncx + ncc
    gd_d = gd_heads * HEAD_DIM
    nhb = gd_heads // hp

    def tile_spec(n, which):
        return pl.BlockSpec((hp, n, HEAD_DIM, GD_CHUNK), lambda i, h: (which * nhb + h, i, 0, 0))

    def row_spec(n, d):
        return pl.BlockSpec((n, hp, 8, GD_CHUNK), lambda i, h: (i, d * nhb + h, 0, 0))

    def dt_spec(n, d):
        return pl.BlockSpec((n, hp, GD_CHUNK, GD_CHUNK), lambda i, h: (i, d * nhb + h, 0, 0))

    def z_spec(n, p):
        return pl.BlockSpec((n, HEAD_DIM), lambda i, h: (i, z_blk + h * hp + p))

    return pl.pallas_call(
        functools.partial(_gdn_kernel, ncc=ncc, ncx=ncx, hp=hp, unroll=unroll),
        grid=(b, nhb),
        in_specs=[tile_spec(ncx, 0), tile_spec(ncx, 1), tile_spec(ncx, 2),
                  tile_spec(ncc, 0), tile_spec(ncc, 1), tile_spec(ncc, 2),
                  row_spec(ncx, 0), row_spec(ncx, 1), row_spec(ncc, 0), row_spec(ncc, 1),
                  dt_spec(ncx, 0), dt_spec(ncx, 1), dt_spec(ncc, 0), dt_spec(ncc, 1)]
                 + [z_spec(t, p) for p in range(hp)] + [z_spec(lc, p) for p in range(hp)]
                 + [pl.BlockSpec((None, 1, HEAD_DIM), lambda i, h: (l, 0, 0))],
        out_specs=[pl.BlockSpec((t, hp * HEAD_DIM), lambda i, h: (i, h)),
                   pl.BlockSpec((lc, hp * HEAD_DIM), lambda i, h: (i, h))],
        out_shape=[jax.ShapeDtypeStruct((b * t, gd_d), BF16),
                   jax.ShapeDtypeStruct((b * lc, gd_d), BF16)],
        scratch_shapes=[pltpu.VMEM((hp, 2, nc, HEAD_DIM, GD_CHUNK), F32),
                        pltpu.VMEM((hp, 2, nc, HEAD_DIM, 2 * GD_CHUNK), BF16),
                        pltpu.VMEM((hp, 2, nc, GD_CHUNK, GD_CHUNK), BF16),
                        pltpu.VMEM((hp, 2, nc, HEAD_DIM, GD_CHUNK), BF16),
                        pltpu.VMEM((hp, nc, HEAD_DIM, GD_CHUNK), F32)],
        compiler_params=_cparams(("parallel", "arbitrary")),
        name="gdn_scan",
    )(gt_x, gt_x, gt_x, gt_c, gt_c, gt_c, rows_x, rows_x, rows_c, rows_c, dt_x, dt_x, dt_c, dt_c,
      *([px] * hp), *([pc] * hp), norm_w)


def _outproj_kernel(x_ref, na_ref, cv_ref, gd_ref, w_ref, gt_ref, lg_ref, lb_ref, o_ref, *, alpha):
    n0 = na_ref.shape[1]
    n1 = n0 + cv_ref.shape[1]
    y = (_dot(na_ref[...], w_ref[0:n0, :]) + _dot(cv_ref[...], w_ref[n0:n1, :])
         + _dot(gd_ref[...], w_ref[n1:, :]))
    z = alpha * x_ref[...] + gt_ref[...] * y
    o_ref[...] = _ln(z) * lg_ref[...] + lb_ref[...]


def _outproj(x, o_na, o_cv, o_gd, w_out, ada, ln_g, ln_b, l, grp, tm, alpha):
    m, d = x.shape
    d_mix = w_out.shape[1]
    return pl.pallas_call(
        functools.partial(_outproj_kernel, alpha=alpha),
        grid=(m // tm,),
        in_specs=[pl.BlockSpec((tm, d), lambda i: (i, 0)),
                  pl.BlockSpec((tm, o_na.shape[1]), lambda i: (i, 0)),
                  pl.BlockSpec((tm, o_cv.shape[1]), lambda i: (i, 0)),
                  pl.BlockSpec((tm, o_gd.shape[1]), lambda i: (i, 0)),
                  pl.BlockSpec((None, d_mix, d), lambda i: (l, 0, 0)),
                  _ada_spec(d, l, grp, 5), _ln_spec(d, l, 1), _ln_spec(d, l, 1)],
        out_specs=pl.BlockSpec((tm, d), lambda i: (i, 0)),
        out_shape=jax.ShapeDtypeStruct((m, d), F32),
        compiler_params=_cparams(("parallel",)),
        name="outproj",
    )(x, o_na, o_cv, o_gd, w_out, ada, ln_g, ln_b)


def _rope_tables(t):
    tok = jnp.arange(t, dtype=jnp.int32)
    row = (tok // GRID_W).astype(F32)
    col = (tok % GRID_W).astype(F32)
    n_freq = HEAD_DIM // 4
    inv_freq = ROPE_BASE ** (-jnp.arange(n_freq, dtype=F32) / n_freq)
    ar = (row[:, None] * inv_freq).T
    ac = (col[:, None] * inv_freq).T
    cos_t = jnp.concatenate([jnp.cos(ar), jnp.cos(ar), jnp.cos(ac), jnp.cos(ac)], axis=0)
    sin_t = jnp.concatenate([-jnp.sin(ar), jnp.sin(ar), -jnp.sin(ac), jnp.sin(ac)], axis=0)
    return cos_t, sin_t


def kernel(x, c, ctx, c_ctx, w_ada, b_ada, ln_g, ln_b, ffn1_w_gu, ffn1_w_down, w_in, na_rpb, cv_conv_w,
           gd_conv_w, gd_a_log, gd_dt_bias, gd_norm_w, w_out, ffn2_w_gu, ffn2_w_down):
    b, t, d = x.shape
    lc = ctx.shape[1]
    depth = w_ada.shape[0]
    d_ff = ffn1_w_down.shape[1]
    na_heads = na_rpb.shape[1]
    win_r, win_c = (na_rpb.shape[2] + 1) // 2, (na_rpb.shape[3] + 1) // 2
    cv_d = cv_conv_w.shape[2]
    cv_groups = cv_d // HEAD_DIM
    gd_heads = gd_a_log.shape[2]
    na_d, gd_d = na_heads * HEAD_DIM, gd_heads * HEAD_DIM
    ndh = 2 * gd_heads
    n_main = 3 * na_d + 3 * cv_d + 3 * gd_d + gd_d
    alpha = (2 * depth) ** 0.25
    assert b + 1 <= 8 and t % GD_CHUNK == 0 and lc % GD_CHUNK == 0 and ndh <= HEAD_DIM // 2
    assert t // GRID_W >= win_r

    tm = math.gcd(math.gcd(t, b * lc), 512)
    tm_big = math.gcd(math.gcd(t, b * lc), 1024)
    tf = 256 if d_ff % 256 == 0 else 128
    tn_ada = 1024 if (N_ADA * d) % 1024 == 0 else 128
    tn_in = math.gcd(n_main, 768)
    gd_hp = 2 if gd_heads % 2 == 0 else 1
    gd_unroll = 4
    na_rq = 4

    w_out_b = w_out.astype(BF16)
    half = HEAD_DIM // 2
    w_in_t = jnp.swapaxes(w_in, 1, 2)
    w_tail_t = jnp.zeros((depth, HEAD_DIM, d), F32)
    w_tail_t = w_tail_t.at[:, 0:ndh].set(w_in_t[:, n_main:n_main + ndh])
    w_tail_t = w_tail_t.at[:, half:half + ndh].set(w_in_t[:, n_main + ndh:n_main + 2 * ndh])
    alog_t = jnp.zeros((depth, HEAD_DIM, GD_CHUNK), F32).at[:, half:half + ndh, :].set(
        jnp.broadcast_to(gd_a_log.reshape(depth, ndh, 1), (depth, ndh, GD_CHUNK)))
    dtb_t = jnp.zeros((depth, HEAD_DIM, GD_CHUNK), F32).at[:, half:half + ndh, :].set(
        jnp.broadcast_to(gd_dt_bias.reshape(depth, ndh, 1), (depth, ndh, GD_CHUNK)))
    ln_g4 = ln_g.reshape(depth, 3, 1, d)
    ln_b4 = ln_b.reshape(depth, 3, 1, d)
    norm_w3 = gd_norm_w.reshape(depth, 1, HEAD_DIM)
    cos_x, sin_x = _rope_tables(t)
    cos_c, sin_c = jnp.ones((HEAD_DIM, lc), F32), jnp.zeros((HEAD_DIM, lc), F32)

    cvec = jnp.zeros((8, d), F32).at[0:b].set(c).at[b].set(c_ctx)
    ada = _ada_table(cvec, w_ada, b_ada, tn_ada).reshape(depth, 8, N_ADA, 1, d)

    grp_x = lambda i: (i * tm) // t
    grp_x_big = lambda i: (i * tm_big) // t
    grp_c = lambda i: b
    ffn1 = lambda v, grp, l: _ffn(v, ada, ln_g4, ln_b4, l, 0, 0, grp, ffn1_w_gu, ffn1_w_down, tm_big, tf, alpha)
    ffn2 = lambda v, grp, l: _ffn(v, ada, ln_g4, ln_b4, l, 6, 2, grp, ffn2_w_gu, ffn2_w_down, tm_big, tf, alpha)

    xs = x.reshape(b * t, d)
    cs = ctx.reshape(b * lc, d)
    cv_blk = 3 * na_heads
    gd_blk = cv_blk + 3 * cv_groups
    z_blk = gd_blk + 3 * gd_heads
    na_rows = t // GRID_W
    assert na_rows % na_rq == 0 and na_rows >= win_r + na_rq and (na_rq * GRID_W) % HEAD_DIM == 0
    na_kinds, na_starts = _na_plan(na_rows, min(win_r, na_rows), win_r, na_rq)
    bias = _na_bias_table(na_rpb, na_kinds, na_rows, win_r, win_c, na_rq)

    for l in range(depth):
        last = l == depth - 1
        xs = ffn1(xs, grp_x_big, l)
        cs = ffn1(cs, grp_c, l)

        p_x, tl_x = _inproj(xs, ada, l, grp_x_big, w_in_t, w_tail_t, n_main, tn_in, tm_big)
        p_c, tl_c = _inproj(cs, ada, l, grp_c, w_in_t, w_tail_t, n_main, tn_in, tm_big)

        o_na_x = _na_attention(p_x, p_c, bias, na_starts, l, b, t, lc, na_heads, win_r, na_rq)
        o_cv_x = _short_conv(p_x, cv_conv_w, l, b, t, cv_groups, cv_blk)

        rows_x, dt_x = _gdn_gates(tl_x, alog_t[l], dtb_t[l], gd_heads)
        rows_c, dt_c = _gdn_gates(tl_c, alog_t[l], dtb_t[l], gd_heads)
        gt_x = _gdn_prep(p_x, gd_conv_w, cos_x, sin_x, l, b, t, gd_blk, gd_heads)
        gt_c = _gdn_prep(p_c, gd_conv_w, cos_c, sin_c, l, b, lc, gd_blk, gd_heads)
        o_gd_x, o_gd_c = _gdn_scan(gt_x, gt_c, rows_x, rows_c, dt_x, dt_c, p_x, p_c, norm_w3, l,
                                   b, t, lc, gd_heads, z_blk, gd_hp, gd_unroll)

        xs = _outproj(xs, o_na_x, o_cv_x, o_gd_x, w_out_b, ada, ln_g4, ln_b4, l, grp_x, tm, alpha)
        xs = ffn2(xs, grp_x_big, l)
        if not last:
            o_na_c = _dense_attention(p_c, b, lc, na_heads)
            o_cv_c = _short_conv(p_c, cv_conv_w, l, b, lc, cv_groups, cv_blk)
            cs = _outproj(cs, o_na_c, o_cv_c, o_gd_c, w_out_b, ada, ln_g4, ln_b4, l, grp_c, tm, alpha)
            cs = ffn2(cs, grp_c, l)
    return xs.reshape(b, t, d)
```

```python
import functools
import math

import jax
import jax.numpy as jnp
from jax import lax
from jax.experimental import pallas as pl
from jax.experimental.pallas import tpu as pltpu

F32 = jnp.float32
BF16 = jnp.bfloat16

HEAD_DIM = 128
GRID_W = 64
GD_CHUNK = 128
ROPE_BASE = 10000.0
LN_EPS = 1e-6
NEG_INF = -1e30
N_ADA = 9
V7X_VMEM_LIMIT = 56 * 1024 * 1024


def _cparams(sem):
    return pltpu.CompilerParams(dimension_semantics=sem, vmem_limit_bytes=V7X_VMEM_LIMIT)


def _ln(x):
    mu = jnp.mean(x, axis=-1, keepdims=True)
    xc = x - mu
    var = jnp.mean(xc * xc, axis=-1, keepdims=True)
    return xc * lax.rsqrt(var + LN_EPS)


def _silu(x):
    return x * jax.nn.sigmoid(x)


def _softplus(x):
    return jnp.maximum(x, 0.0) + jnp.log(1.0 + jnp.exp(-jnp.abs(x)))


def _dot(a, b):
    return jnp.dot(a, b, preferred_element_type=F32)


def _dot_nt(a, b):
    return lax.dot_general(a, b, (((1,), (1,)), ((), ())), preferred_element_type=F32)


def _dot_tn(a, b):
    return lax.dot_general(a, b, (((0,), (0,)), ((), ())), preferred_element_type=F32)


def _ada_kernel(c_ref, w_ref, b_ref, o_ref):
    s = _silu(c_ref[...]).astype(BF16)
    o_ref[...] = _dot(s, w_ref[...].astype(BF16)) + b_ref[...]


def _ada_table(cvec, w_ada, b_ada, tn):
    depth, d, n = w_ada.shape
    return pl.pallas_call(
        _ada_kernel,
        grid=(depth, n // tn),
        in_specs=[pl.BlockSpec((8, d), lambda l, j: (0, 0)),
                  pl.BlockSpec((None, d, tn), lambda l, j: (l, 0, j)),
                  pl.BlockSpec((None, 1, tn), lambda l, j: (l, 0, j))],
        out_specs=pl.BlockSpec((None, 8, tn), lambda l, j: (l, 0, j)),
        out_shape=jax.ShapeDtypeStruct((depth, 8, n), F32),
        compiler_params=_cparams(("parallel", "parallel")),
        name="ada_table",
    )(cvec, w_ada, b_ada.reshape(depth, 1, n))


def _ada_spec(d, l, grp, j):
    return pl.BlockSpec((None, None, None, 1, d), lambda i, *_: (l, grp(i), j, 0, 0))


def _ln_spec(d, l, k):
    return pl.BlockSpec((None, None, 1, d), lambda *_: (l, k, 0, 0))


def _ffn_kernel(*refs, alpha, emit, ahead, slab):
    refs = list(refs)
    x_ref, sh_ref, sc_ref, gt_ref = refs[:4]
    del refs[:4]
    if ahead:
        xn_ref, shn_ref, scn_ref = refs[:3]
        del refs[:3]
    wg_ref, wu_ref, wd_ref, lg_ref, lb_ref, o_ref = refs[:6]
    del refs[:6]
    if emit:
        wgb_ref, wub_ref, wdb_ref = refs[:3]
        del refs[:3]
    h_refs = refs
    i, f = pl.program_id(0), pl.program_id(1)
    tm = x_ref.shape[0]

    def modulate(xv, sc, sh):
        return (_ln(xv) * (1.0 + sc) + sh).astype(BF16)

    @pl.when(f == 0)
    def _():
        o_ref[...] = jnp.zeros_like(o_ref)

    @pl.when((f == 0) & (i == 0) if ahead else f == 0)
    def _():
        h_refs[0][...] = modulate(x_ref[...], sc_ref[...], sh_ref[...])

    def step(h_cur, h_nxt):
        wg, wu, wd = wg_ref[...].astype(BF16), wu_ref[...].astype(BF16), wd_ref[...].astype(BF16)
        if emit:
            wgb_ref[...] = wg
            wub_ref[...] = wu
            wdb_ref[...] = wd
        h = h_cur[...]
        a = (_silu(_dot(h, wg)) * _dot(h, wu)).astype(BF16)
        o_ref[...] += _dot(a, wd)
        if h_nxt is not None:
            rows = pl.ds(pl.multiple_of(jnp.minimum(f * slab, tm - slab), 16), slab)
            h_nxt[rows, :] = modulate(xn_ref[rows, :], scn_ref[...], shn_ref[...])

    if ahead:
        pl.when(i % 2 == 0)(lambda: step(h_refs[0], h_refs[1]))
        pl.when(i % 2 == 1)(lambda: step(h_refs[1], h_refs[0]))
    else:
        step(h_refs[0], None)

    @pl.when(f == pl.num_programs(1) - 1)
    def _():
        z = alpha * x_ref[...] + (0.5 * gt_ref[...]) * o_ref[...]
        o_ref[...] = _ln(z) * lg_ref[...] + lb_ref[...]


def _ffn(x, ada, ln_g, ln_b, l, j0, k_ln, grp, weights, tm, tf, alpha, emit=False, ahead=False):
    w_gate, w_up, up_col, w_down, wl = weights
    m, d = x.shape
    d_ff = w_down.shape[-2]
    nf = d_ff // tf
    nt = m // tm
    up_blk = up_col // tf
    lead = () if wl is None else (None,)
    at = (lambda *idx: idx) if wl is None else (lambda *idx: (wl,) + idx)
    nxt = lambda i: jnp.minimum(i + 1, nt - 1)
    slab = -(-pl.cdiv(tm, nf) // 16) * 16
    assert (not ahead) or (slab * nf >= tm and tm % 16 == 0 and slab <= tm)
    single = w_gate.dtype == F32
    x_spec = (pl.BlockSpec((tm, d), lambda i, f: (i, 0), pipeline_mode=pl.Buffered(1)) if single
              else pl.BlockSpec((tm, d), lambda i, f: (i, 0)))
    in_specs = [x_spec, _ada_spec(d, l, grp, j0), _ada_spec(d, l, grp, j0 + 1), _ada_spec(d, l, grp, j0 + 2)]
    args = [x, ada, ada, ada]
    if ahead:
        grp_n = lambda i: grp(nxt(i))
        in_specs += [pl.BlockSpec((tm, d), lambda i, f: (nxt(i), 0)),
                     _ada_spec(d, l, grp_n, j0), _ada_spec(d, l, grp_n, j0 + 1)]
        args += [x, ada, ada]
    in_specs += [pl.BlockSpec(lead + (d, tf), lambda i, f: at(0, f)),
                 pl.BlockSpec(lead + (d, tf), lambda i, f: at(0, f + up_blk)),
                 pl.BlockSpec(lead + (tf, d), lambda i, f: at(f, 0)),
                 _ln_spec(d, l, k_ln), _ln_spec(d, l, k_ln)]
    args += [w_gate, w_up, w_down, ln_g, ln_b]
    out_specs = [pl.BlockSpec((tm, d), lambda i, f: (i, 0))]
    out_shape = [jax.ShapeDtypeStruct((m, d), F32)]
    if emit:
        out_specs += [pl.BlockSpec((d, tf), lambda i, f: (0, f)), pl.BlockSpec((d, tf), lambda i, f: (0, f)),
                      pl.BlockSpec((tf, d), lambda i, f: (f, 0))]
        out_shape += [jax.ShapeDtypeStruct((d, d_ff), BF16), jax.ShapeDtypeStruct((d, d_ff), BF16),
                      jax.ShapeDtypeStruct((d_ff, d), BF16)]
    res = pl.pallas_call(
        functools.partial(_ffn_kernel, alpha=alpha, emit=emit, ahead=ahead, slab=slab),
        grid=(nt, nf),
        in_specs=in_specs,
        out_specs=out_specs,
        out_shape=out_shape,
        scratch_shapes=[pltpu.VMEM((tm, d), BF16)] * (2 if ahead else 1),
        compiler_params=_cparams(("arbitrary", "arbitrary")),
        name="ffn",
    )(*args)
    return res if emit else res[0]


def _inproj_kernel(x_ref, sh_ref, sc_ref, w_ref, wt_ref, o_ref, t_ref, h_ref):
    @pl.when(pl.program_id(1) == 0)
    def _():
        h = _ln(x_ref[...]) * (1.0 + sc_ref[...]) + sh_ref[...]
        h_ref[...] = h.astype(BF16)
        t_ref[...] = _dot_nt(h_ref[...], wt_ref[...].astype(BF16))

    o_ref[...] = _dot_nt(h_ref[...], w_ref[...].astype(BF16)).astype(o_ref.dtype)


def _inproj(x, ada, l, grp, w_in_t, w_tail_t, n_main, tn, tm):
    m, d = x.shape
    return pl.pallas_call(
        _inproj_kernel,
        grid=(m // tm, n_main // tn),
        in_specs=[pl.BlockSpec((tm, d), lambda i, n: (i, 0)),
                  _ada_spec(d, l, grp, 3), _ada_spec(d, l, grp, 4),
                  pl.BlockSpec((None, tn, d), lambda i, n: (l, n, 0)),
                  pl.BlockSpec((None, HEAD_DIM, d), lambda i, n: (l, 0, 0))],
        out_specs=[pl.BlockSpec((tm, tn), lambda i, n: (i, n)),
                   pl.BlockSpec((tm, HEAD_DIM), lambda i, n: (i, 0))],
        out_shape=[jax.ShapeDtypeStruct((m, n_main), BF16),
                   jax.ShapeDtypeStruct((m, HEAD_DIM), F32)],
        scratch_shapes=[pltpu.VMEM((tm, d), BF16)],
        compiler_params=_cparams(("parallel", "arbitrary")),
        name="inproj",
    )(x, ada, ada, w_in_t, w_tail_t)


LOG2_E = 1.4426950408889634


def _softmax_pv(s_parts, v_parts, scale):
    m = functools.reduce(jnp.maximum, [jnp.max(s, axis=-1, keepdims=True) for s in s_parts])
    ps = [jnp.exp2((s - m) * (scale * LOG2_E)) for s in s_parts]
    den = functools.reduce(lambda a, b: a + b, [jnp.sum(p, axis=-1, keepdims=True) for p in ps])
    o = functools.reduce(lambda a, b: a + b, [_dot(p.astype(BF16), v) for p, v in zip(ps, v_parts)])
    return o / den


def _na_plan(rows, wr, win_r, rq):
    band = lambda r: min(max(r - wr // 2, 0), rows - wr)
    kinds, starts = [], []
    for rb in range(rows // rq):
        r0 = rb * rq
        u0 = min(band(r0), rows - (wr + rq))
        kind = tuple((band(r0 + i) - (r0 + i) + win_r - 1, band(r0 + i) - u0) for i in range(rq))
        if not kinds or kinds[-1] != kind:
            kinds.append(kind)
            starts.append(rb)
    return kinds, starts


def _na_kernel(q_ref, k_ref, v_ref, kc_ref, vc_ref, b_ref, o_ref, *, heads, rows, wr, rq, scale):
    r0 = pl.program_id(1) * rq
    nbr = wr + rq
    u0 = jnp.minimum(jnp.clip(r0 - wr // 2, 0, rows - wr), rows - nbr)
    start = pl.multiple_of(u0 * GRID_W, GRID_W)
    nb = nbr * GRID_W

    def scores(h):
        cs = slice(h * HEAD_DIM, (h + 1) * HEAD_DIM)
        q = q_ref[:, cs]
        return [_dot_nt(q, k_ref[pl.ds(start, nb), cs]) + b_ref[h], _dot_nt(q, kc_ref[:, cs])]

    pending = scores(0)
    for h in range(heads):
        nxt = scores(h + 1) if h + 1 < heads else None
        cs = slice(h * HEAD_DIM, (h + 1) * HEAD_DIM)
        o = _softmax_pv(pending, [v_ref[pl.ds(start, nb), cs], vc_ref[:, cs]], scale)
        o_ref[:, cs] = o.astype(o_ref.dtype)
        pending = nxt


def _na_attention(px, pc, bias, starts, l, b, t, lc, heads, win_r, rq):
    na_d = heads * HEAD_DIM
    rows = t // GRID_W
    wr = min(win_r, rows)
    nrb = rows // rq
    nb = (wr + rq) * GRID_W

    def kind(r):
        return sum((r >= s0).astype(jnp.int32) for s0 in starts[1:]) if len(starts) > 1 else 0

    return pl.pallas_call(
        functools.partial(_na_kernel, heads=heads, rows=rows, wr=wr, rq=rq, scale=HEAD_DIM ** -0.5),
        grid=(b, nrb),
        in_specs=[pl.BlockSpec((rq * GRID_W, na_d), lambda i, r: (i * nrb + r, 0)),
                  pl.BlockSpec((t, na_d), lambda i, r: (i, 1)),
                  pl.BlockSpec((t, na_d), lambda i, r: (i, 2)),
                  pl.BlockSpec((lc, na_d), lambda i, r: (i, 1)),
                  pl.BlockSpec((lc, na_d), lambda i, r: (i, 2)),
                  pl.BlockSpec((None, heads, None, rq * GRID_W, nb), lambda i, r: (l, 0, kind(r), 0, 0))],
        out_specs=pl.BlockSpec((rq * GRID_W, na_d), lambda i, r: (i * nrb + r, 0)),
        out_shape=jax.ShapeDtypeStruct((b * t, na_d), BF16),
        compiler_params=_cparams(("parallel", "arbitrary")),
        name="na_attention",
    )(px, px, px, pc, pc, bias)


def _dense_attn_kernel(q_ref, k_ref, v_ref, o_ref, *, heads, scale):
    for h in range(heads):
        cs = slice(h * HEAD_DIM, (h + 1) * HEAD_DIM)
        s = _dot_nt(q_ref[:, cs], k_ref[:, cs])
        o_ref[:, cs] = _softmax_pv([s], [v_ref[:, cs]], scale).astype(o_ref.dtype)


def _dense_attention(pc, b, lc, heads):
    na_d = heads * HEAD_DIM
    return pl.pallas_call(
        functools.partial(_dense_attn_kernel, heads=heads, scale=HEAD_DIM ** -0.5),
        grid=(b,),
        in_specs=[pl.BlockSpec((lc, na_d), lambda i: (i, 0)),
                  pl.BlockSpec((lc, na_d), lambda i: (i, 1)),
                  pl.BlockSpec((lc, na_d), lambda i: (i, 2))],
        out_specs=pl.BlockSpec((lc, na_d), lambda i: (i, 0)),
        out_shape=jax.ShapeDtypeStruct((b * lc, na_d), BF16),
        compiler_params=_cparams(("parallel",)),
        name="ctx_attention",
    )(pc, pc, pc)


def _na_bias_kernel(r_ref, o_ref, *, win_r, win_c, wr, kinds, inv_scale):
    n_dr, n_dc = 2 * win_r - 1, 2 * win_c - 1
    base = (pl.program_id(0) * pl.num_programs(1) + pl.program_id(1)) * (n_dr * n_dc)
    q = lax.broadcasted_iota(jnp.int32, (GRID_W, GRID_W), 0)
    k = lax.broadcasted_iota(jnp.int32, (GRID_W, GRID_W), 1)
    dc = jnp.clip(k - q + win_c - 1, 0, n_dc - 1)
    col_start = jnp.clip(q - win_c // 2, 0, GRID_W - win_c)
    col_ok = (k >= col_start) & (k < col_start + win_c)
    o_ref[...] = jnp.full(o_ref.shape, NEG_INF, F32)
    for a in range(n_dr):
        acc = jnp.zeros((GRID_W, GRID_W), F32)
        for j in range(n_dc):
            acc = jnp.where(dc == j, r_ref[base + a * n_dc + j], acc)
        g = jnp.where(col_ok, acc * inv_scale, NEG_INF)
        for kd, kind in enumerate(kinds):
            for i, (dr0, off) in enumerate(kind):
                w = a - dr0
                if 0 <= w < wr:
                    o_ref[kd, i * GRID_W:(i + 1) * GRID_W, (off + w) * GRID_W:(off + w + 1) * GRID_W] = g


def _na_bias_table(rpb, kinds, rows, win_r, win_c, rq):
    depth, heads = rpb.shape[0], rpb.shape[1]
    wr = min(win_r, rows)
    shape = (len(kinds), rq * GRID_W, (wr + rq) * GRID_W)
    return pl.pallas_call(
        functools.partial(_na_bias_kernel, win_r=win_r, win_c=win_c, wr=wr, kinds=kinds, inv_scale=HEAD_DIM ** 0.5),
        grid=(depth, heads),
        in_specs=[pl.BlockSpec(memory_space=pltpu.SMEM)],
        out_specs=pl.BlockSpec((None, None) + shape, lambda l, h: (l, h, 0, 0, 0)),
        out_shape=jax.ShapeDtypeStruct((depth, heads) + shape, F32),
        compiler_params=_cparams(("parallel", "parallel")),
        name="na_bias",
    )(rpb.reshape(-1))


SUBLANES = 8


def _dwconv_rows(x, w, pad_ref):
    ksz, t = w.shape[0], x.shape[0]
    assert ksz // 2 <= SUBLANES
    halo = jnp.zeros((SUBLANES, x.shape[1]), F32)
    pad_ref[0:SUBLANES, :] = halo
    pad_ref[SUBLANES + t:2 * SUBLANES + t, :] = halo
    pad_ref[SUBLANES:SUBLANES + t, :] = x
    y = None
    for j in range(ksz):
        start = SUBLANES + j - ksz // 2
        term = w[j:j + 1, :] * pad_ref[start:start + t, :]
        y = term if y is None else y + term
    return y


def _cv_kernel(b_ref, c_ref, u_ref, w_ref, o_ref, pad_ref):
    cu = c_ref[...].astype(F32) * u_ref[...].astype(F32)
    o_ref[...] = (b_ref[...].astype(F32) * _dwconv_rows(cu, w_ref[...], pad_ref)).astype(o_ref.dtype)


def _short_conv(p, cv_w, l, nseq, tseq, cv_groups, col0_blk):
    ksz = cv_w.shape[1]
    cv_d = cv_groups * HEAD_DIM
    return pl.pallas_call(
        _cv_kernel,
        grid=(nseq, cv_groups),
        in_specs=[pl.BlockSpec((tseq, HEAD_DIM), lambda s, j: (s, col0_blk + j)),
                  pl.BlockSpec((tseq, HEAD_DIM), lambda s, j: (s, col0_blk + cv_groups + j)),
                  pl.BlockSpec((tseq, HEAD_DIM), lambda s, j: (s, col0_blk + 2 * cv_groups + j)),
                  pl.BlockSpec((None, ksz, HEAD_DIM), lambda s, j: (l, 0, j))],
        out_specs=pl.BlockSpec((tseq, HEAD_DIM), lambda s, j: (s, j)),
        out_shape=jax.ShapeDtypeStruct((nseq * tseq, cv_d), BF16),
        scratch_shapes=[pltpu.VMEM((tseq + 2 * SUBLANES, HEAD_DIM), F32)],
        compiler_params=_cparams(("parallel", "parallel")),
        name="short_conv",
    )(p, p, p, cv_w)


def _gdn_prep_kernel(x_ref, w_ref, cos_ref, sin_ref, o_ref, pad_ref, *, gd_heads):
    j = pl.program_id(1)
    y = _dwconv_rows(x_ref[...].astype(F32), w_ref[...], pad_ref)
    st = _silu(y).T

    def emit(res):
        for c in range(o_ref.shape[0]):
            o_ref[c] = res[:, c * GD_CHUNK:(c + 1) * GD_CHUNK]

    @pl.when(j < 2 * gd_heads)
    def _():
        ss = jnp.sum(st * st, axis=0, keepdims=True)
        nrm = st * (lax.rsqrt(ss + LN_EPS) * jnp.where(j < gd_heads, HEAD_DIM ** -0.5, 1.0))
        q4 = HEAD_DIM // 4
        swapped = jnp.concatenate([nrm[q4:2 * q4], nrm[0:q4], nrm[3 * q4:4 * q4], nrm[2 * q4:3 * q4]], axis=0)
        emit(nrm * cos_ref[...] + swapped * sin_ref[...])

    @pl.when(j >= 2 * gd_heads)
    def _():
        emit(st)


def _gdn_prep(p, conv_w, cos_t, sin_t, l, nseq, tseq, col0_blk, gd_heads):
    ksz = conv_w.shape[1]
    nch = tseq // GD_CHUNK
    return pl.pallas_call(
        functools.partial(_gdn_prep_kernel, gd_heads=gd_heads),
        grid=(nseq, 3 * gd_heads),
        in_specs=[pl.BlockSpec((tseq, HEAD_DIM), lambda s, j: (s, col0_blk + j)),
                  pl.BlockSpec((None, ksz, HEAD_DIM), lambda s, j: (l, 0, j)),
                  pl.BlockSpec((HEAD_DIM, tseq), lambda s, j: (0, 0)),
                  pl.BlockSpec((HEAD_DIM, tseq), lambda s, j: (0, 0))],
        out_specs=pl.BlockSpec((None, nch, HEAD_DIM, GD_CHUNK), lambda s, j: (j, s, 0, 0)),
        out_shape=jax.ShapeDtypeStruct((3 * gd_heads, nseq * nch, HEAD_DIM, GD_CHUNK), F32),
        scratch_shapes=[pltpu.VMEM((tseq + 2 * SUBLANES, HEAD_DIM), F32)],
        compiler_params=_cparams(("parallel", "parallel")),
        name="gdn_prep",
    )(p, conv_w, cos_t, sin_t)


def _gdn_gate_kernel(t_ref, alog_ref, dtb_ref, rows_ref, dt_ref, *, gd_heads):
    ndh = 2 * gd_heads
    half = HEAD_DIM // 2
    a = t_ref[...].T
    beta = jax.nn.sigmoid(a)
    g = -jnp.exp(alog_ref[...]) * _softplus(a + dtb_ref[...])
    ri = lax.broadcasted_iota(jnp.int32, (GD_CHUNK, GD_CHUNK), 0)
    ci = lax.broadcasted_iota(jnp.int32, (GD_CHUNK, GD_CHUNK), 1)
    g1 = g.astype(BF16)
    r1 = g - g1.astype(F32)
    g2 = r1.astype(BF16)
    g3 = (r1 - g2.astype(F32)).astype(BF16)

    def cumsum(tri):
        tb = jnp.where(tri, 1.0, 0.0).astype(BF16)
        return _dot(g1, tb) + _dot(g2, tb) + _dot(g3, tb)

    gc = jnp.where(ri < half + gd_heads, cumsum(ri <= ci), cumsum(ri >= ci))
    glast = jnp.sum(g, axis=1, keepdims=True)
    e = jnp.exp(gc)
    f = jnp.exp(glast - gc)
    gt = jnp.exp(glast) + jnp.zeros_like(gc)
    gct = gc.T
    zero_rows = jnp.zeros((3, GD_CHUNK), F32)
    for dh in range(ndh):
        r = half + dh
        rows_ref[dh, 0:1, :] = beta[dh:dh + 1]
        rows_ref[dh, 1:2, :] = beta[dh:dh + 1] * e[r:r + 1]
        rows_ref[dh, 2:3, :] = f[r:r + 1]
        rows_ref[dh, 3:4, :] = gt[r:r + 1]
        rows_ref[dh, 4:5, :] = e[r:r + 1]
        rows_ref[dh, 5:8, :] = zero_rows
        col = jnp.sum(jnp.where(ci == r, gct, 0.0), axis=1, keepdims=True)
        diff = gc[r:r + 1] - col
        mask = (ci >= ri) if dh < gd_heads else (ci <= ri)
        dt_ref[dh] = jnp.where(mask, jnp.exp(jnp.where(mask, diff, 0.0)), 0.0)


def _gdn_gates(tail, alog_t, dtb_t, gd_heads):
    m = tail.shape[0]
    nch = m // GD_CHUNK
    ndh = 2 * gd_heads
    return pl.pallas_call(
        functools.partial(_gdn_gate_kernel, gd_heads=gd_heads),
        grid=(nch,),
        in_specs=[pl.BlockSpec((GD_CHUNK, HEAD_DIM), lambda n: (n, 0)),
                  pl.BlockSpec((HEAD_DIM, GD_CHUNK), lambda n: (0, 0)),
                  pl.BlockSpec((HEAD_DIM, GD_CHUNK), lambda n: (0, 0))],
        out_specs=[pl.BlockSpec((None, ndh, 8, GD_CHUNK), lambda n: (n, 0, 0, 0)),
                   pl.BlockSpec((None, ndh, GD_CHUNK, GD_CHUNK), lambda n: (n, 0, 0, 0))],
        out_shape=[jax.ShapeDtypeStruct((nch, ndh, 8, GD_CHUNK), F32),
                   jax.ShapeDtypeStruct((nch, ndh, GD_CHUNK, GD_CHUNK), F32)],
        compiler_params=_cparams(("parallel",)),
        name="gdn_gates",
    )(tail, alog_t, dtb_t)


def _tri_inverse_multi(mnegs, ri, ci):
    eye = jnp.where(ri == ci, 1.0, 0.0)

    def blk(n):
        return (ri // n) == (ci // n)

    b16 = blk(16)
    qs = [jnp.where(b16, m, 0.0) for m in mnegs]
    xs = [eye + q for q in qs]
    qbs = [q.astype(BF16) for q in qs]
    qs = [_dot(qb, qb) for qb in qbs]
    for _ in range(2):
        qbs = [q.astype(BF16) for q in qs]
        xq = [_dot(x.astype(BF16), qb) for x, qb in zip(xs, qbs)]
        qs = [_dot(qb, qb) for qb in qbs]
        xs = [x + d for x, d in zip(xs, xq)]
    xq = [_dot(x.astype(BF16), q.astype(BF16)) for x, q in zip(xs, qs)]
    xs = [x + d for x, d in zip(xs, xq)]
    for n in (32, 64, 128):
        msk = blk(n) & jnp.logical_not(blk(n // 2))
        offs = [jnp.where(msk, m, 0.0).astype(BF16) for m in mnegs]
        xbs = [x.astype(BF16) for x in xs]
        t1 = [_dot(xb, off).astype(BF16) for xb, off in zip(xbs, offs)]
        t2 = [_dot(t, xb) for t, xb in zip(t1, xbs)]
        xs = [x + d for x, d in zip(xs, t2)]
    return xs


def _gdn_kernel(*refs, ncc, ncx, hp, unroll):
    (qx, kx, vx, qc, kc, vc, rx0, rx1, rc0, rc1, dx0, dx1, dc0, dc1) = refs[:14]
    zx = refs[14:14 + hp]
    zc = refs[14 + hp:14 + 2 * hp]
    nw, ox, oc, ut_s, wq_s, at_s, kf_s, o_s = refs[14 + 2 * hp:]
    ri = lax.broadcasted_iota(jnp.int32, (GD_CHUNK, GD_CHUNK), 0)
    ci = lax.broadcasted_iota(jnp.int32, (GD_CHUNK, GD_CHUNK), 1)
    strict = (ci > ri, ci < ri)

    def local(units, base, q_ref, k_ref, v_ref, rows, dts):
        kts = [k_ref[p, i] for p, i in units]
        kbs = [kt.astype(BF16) for kt in kts]
        kks = [_dot_tn(kb, kb) for kb in kbs]
        qks = [_dot_tn(kb, q_ref[p, i].astype(BF16)) for kb, (p, i) in zip(kbs, units)]
        chains = [(u, d) for u in range(len(units)) for d in range(2)]
        mnegs = []
        for u, d in chains:
            p, i = units[u]
            mnegs.append(-(jnp.where(strict[d], dts[d][i, p], 0.0) * kks[u] * rows[d][i, p][0:1]))
        tts = _tri_inverse_multi(mnegs, ri, ci)
        sols = []
        for (u, d), tt in zip(chains, tts):
            p, i = units[u]
            rw = rows[d][i, p]
            rhs = jnp.concatenate([v_ref[p, i] * rw[0:1], kts[u] * rw[1:2]], axis=0).astype(BF16)
            sols.append(_dot(rhs, tt.astype(BF16)))
        for (u, d), sol in zip(chains, sols):
            p, i = units[u]
            n = base + i
            rw = rows[d][i, p]
            ut_s[p, d, n] = sol[:HEAD_DIM]
            wq_s[p, d, n, :, 0:GD_CHUNK] = sol[HEAD_DIM:].astype(BF16)
            wq_s[p, d, n, :, GD_CHUNK:2 * GD_CHUNK] = (q_ref[p, i] * rw[4:5]).astype(BF16)
            at_s[p, d, n] = (qks[u] * dts[d][i, p]).astype(BF16)
            kf_s[p, d, n] = (kts[u] * rw[2:3]).astype(BF16)
        for p, i in units:
            o_s[p, base + i] = jnp.zeros((HEAD_DIM, GD_CHUNK), F32)

    def seg_local(q_ref, k_ref, v_ref, rows, dts, count, base):
        per = min(unroll, count)

        def body(it, carry):
            units = [(p, it * per + u) for u in range(per) for p in range(hp)]
            local(units, base, q_ref, k_ref, v_ref, rows, dts)
            return carry
        lax.fori_loop(0, count // per, body, 0)

    seg_local(qc, kc, vc, (rc0, rc1), (dc0, dc1), ncc, 0)
    seg_local(qx, kx, vx, (rx0, rx1), (dx0, dx1), ncx, ncc)

    def seg_scan(states, rows, count, base):
        def body(i, carry):
            j = count - 1 - i
            chains = [(p, d, (i, j)[d]) for p in range(hp) for d in range(2)]
            ys = [_dot(st.astype(BF16), wq_s[p, d, base + c]) for st, (p, d, c) in zip(carry, chains)]
            vnbs = [(ut_s[p, d, base + c] - y[:, 0:GD_CHUNK]).astype(BF16) for y, (p, d, c) in zip(ys, chains)]
            ots = [_dot(vnb, at_s[p, d, base + c]) for vnb, (p, d, c) in zip(vnbs, chains)]
            upd = [_dot_nt(vnb, kf_s[p, d, base + c]) for vnb, (p, d, c) in zip(vnbs, chains)]
            for y, ot, (p, d, c) in zip(ys, ots, chains):
                o_s[p, base + c] += y[:, GD_CHUNK:2 * GD_CHUNK] + ot
            return tuple(st * rows[d][c, p][3:4] + du for st, du, (p, d, c) in zip(carry, upd, chains))
        return lax.fori_loop(0, count, body, states)

    zero = jnp.zeros((HEAD_DIM, HEAD_DIM), F32)
    states = seg_scan((zero,) * (2 * hp), (rc0, rc1), ncc, 0)
    seg_scan(states, (rx0, rx1), ncx, ncc)

    def seg_out(o_ref, z_refs, count, base):
        def body(i, carry):
            rs = pl.ds(pl.multiple_of(i * GD_CHUNK, GD_CHUNK), GD_CHUNK)
            for p in range(hp):
                ot = o_s[p, base + i]
                ms = jnp.mean(ot * ot, axis=0, keepdims=True)
                on = (ot * lax.rsqrt(ms + LN_EPS)).T
                z = z_refs[p][rs, :].astype(F32)
                o_ref[rs, p * HEAD_DIM:(p + 1) * HEAD_DIM] = (on * nw[...] * _silu(z)).astype(o_ref.dtype)
            return carry
        lax.fori_loop(0, count, body, 0)

    seg_out(oc, zc, ncc, 0)
    seg_out(ox, zx, ncx, ncc)


def _gdn_scan(gt_x, gt_c, rows_x, rows_c, dt_x, dt_c, px, pc, norm_w, l, b, t, lc, gd_heads, z_blk, hp, unroll):
    ncx, ncc = t // GD_CHUNK, lc // GD_CHUNK
    nc = ncx + ncc
    gd_d = gd_heads * HEAD_DIM
    nhb = gd_heads // hp

    def tile_spec(n, which):
        return pl.BlockSpec((hp, n, HEAD_DIM, GD_CHUNK), lambda i, h: (which * nhb + h, i, 0, 0))

    def row_spec(n, d):
        return pl.BlockSpec((n, hp, 8, GD_CHUNK), lambda i, h: (i, d * nhb + h, 0, 0))

    def dt_spec(n, d):
        return pl.BlockSpec((n, hp, GD_CHUNK, GD_CHUNK), lambda i, h: (i, d * nhb + h, 0, 0))

    def z_spec(n, p):
        return pl.BlockSpec((n, HEAD_DIM), lambda i, h: (i, z_blk + h * hp + p))

    return pl.pallas_call(
        functools.partial(_gdn_kernel, ncc=ncc, ncx=ncx, hp=hp, unroll=unroll),
        grid=(b, nhb),
        in_specs=[tile_spec(ncx, 0), tile_spec(ncx, 1), tile_spec(ncx, 2),
                  tile_spec(ncc, 0), tile_spec(ncc, 1), tile_spec(ncc, 2),
                  row_spec(ncx, 0), row_spec(ncx, 1), row_spec(ncc, 0), row_spec(ncc, 1),
                  dt_spec(ncx, 0), dt_spec(ncx, 1), dt_spec(ncc, 0), dt_spec(ncc, 1)]
                 + [z_spec(t, p) for p in range(hp)] + [z_spec(lc, p) for p in range(hp)]
                 + [pl.BlockSpec((None, 1, HEAD_DIM), lambda i, h: (l, 0, 0))],
        out_specs=[pl.BlockSpec((t, hp * HEAD_DIM), lambda i, h: (i, h)),
                   pl.BlockSpec((lc, hp * HEAD_DIM), lambda i, h: (i, h))],
        out_shape=[jax.ShapeDtypeStruct((b * t, gd_d), BF16),
                   jax.ShapeDtypeStruct((b * lc, gd_d), BF16)],
        scratch_shapes=[pltpu.VMEM((hp, 2, nc, HEAD_DIM, GD_CHUNK), F32),
                        pltpu.VMEM((hp, 2, nc, HEAD_DIM, 2 * GD_CHUNK), BF16),
                        pltpu.VMEM((hp, 2, nc, GD_CHUNK, GD_CHUNK), BF16),
                        pltpu.VMEM((hp, 2, nc, HEAD_DIM, GD_CHUNK), BF16),
                        pltpu.VMEM((hp, nc, HEAD_DIM, GD_CHUNK), F32)],
        compiler_params=_cparams(("parallel", "arbitrary")),
        name="gdn_scan",
    )(gt_x, gt_x, gt_x, gt_c, gt_c, gt_c, rows_x, rows_x, rows_c, rows_c, dt_x, dt_x, dt_c, dt_c,
      *([px] * hp), *([pc] * hp), norm_w)


def _outproj_kernel(x_ref, na_ref, cv_ref, gd_ref, w_ref, gt_ref, lg_ref, lb_ref, o_ref, *, alpha):
    n0 = na_ref.shape[1]
    n1 = n0 + cv_ref.shape[1]
    y = (_dot(na_ref[...], w_ref[0:n0, :]) + _dot(cv_ref[...], w_ref[n0:n1, :])
         + _dot(gd_ref[...], w_ref[n1:, :]))
    z = alpha * x_ref[...] + gt_ref[...] * y
    o_ref[...] = _ln(z) * lg_ref[...] + lb_ref[...]


def _outproj(x, o_na, o_cv, o_gd, w_out, ada, ln_g, ln_b, l, grp, tm, alpha):
    m, d = x.shape
    d_mix = w_out.shape[1]
    return pl.pallas_call(
        functools.partial(_outproj_kernel, alpha=alpha),
        grid=(m // tm,),
        in_specs=[pl.BlockSpec((tm, d), lambda i: (i, 0)),
                  pl.BlockSpec((tm, o_na.shape[1]), lambda i: (i, 0)),
                  pl.BlockSpec((tm, o_cv.shape[1]), lambda i: (i, 0)),
                  pl.BlockSpec((tm, o_gd.shape[1]), lambda i: (i, 0)),
                  pl.BlockSpec((None, d_mix, d), lambda i: (l, 0, 0)),
                  _ada_spec(d, l, grp, 5), _ln_spec(d, l, 1), _ln_spec(d, l, 1)],
        out_specs=pl.BlockSpec((tm, d), lambda i: (i, 0)),
        out_shape=jax.ShapeDtypeStruct((m, d), F32),
        compiler_params=_cparams(("parallel",)),
        name="outproj",
    )(x, o_na, o_cv, o_gd, w_out, ada, ln_g, ln_b)


def _rope_tables(t):
    tok = jnp.arange(t, dtype=jnp.int32)
    row = (tok // GRID_W).astype(F32)
    col = (tok % GRID_W).astype(F32)
    n_freq = HEAD_DIM // 4
    inv_freq = ROPE_BASE ** (-jnp.arange(n_freq, dtype=F32) / n_freq)
    ar = (row[:, None] * inv_freq).T
    ac = (col[:, None] * inv_freq).T
    cos_t = jnp.concatenate([jnp.cos(ar), jnp.cos(ar), jnp.cos(ac), jnp.cos(ac)], axis=0)
    sin_t = jnp.concatenate([-jnp.sin(ar), jnp.sin(ar), -jnp.sin(ac), jnp.sin(ac)], axis=0)
    return cos_t, sin_t


def kernel(x, c, ctx, c_ctx, w_ada, b_ada, ln_g, ln_b, ffn1_w_gu, ffn1_w_down, w_in, na_rpb, cv_conv_w,
           gd_conv_w, gd_a_log, gd_dt_bias, gd_norm_w, w_out, ffn2_w_gu, ffn2_w_down):
    b, t, d = x.shape
    lc = ctx.shape[1]
    depth = w_ada.shape[0]
    d_ff = ffn1_w_down.shape[1]
    na_heads = na_rpb.shape[1]
    win_r, win_c = (na_rpb.shape[2] + 1) // 2, (na_rpb.shape[3] + 1) // 2
    cv_d = cv_conv_w.shape[2]
    cv_groups = cv_d // HEAD_DIM
    gd_heads = gd_a_log.shape[2]
    na_d, gd_d = na_heads * HEAD_DIM, gd_heads * HEAD_DIM
    ndh = 2 * gd_heads
    n_main = 3 * na_d + 3 * cv_d + 3 * gd_d + gd_d
    alpha = (2 * depth) ** 0.25
    assert b + 1 <= 8 and t % GD_CHUNK == 0 and lc % GD_CHUNK == 0 and ndh <= HEAD_DIM // 2
    assert t // GRID_W >= win_r

    tm = math.gcd(math.gcd(t, b * lc), 512)
    tm_big = math.gcd(math.gcd(t, b * lc), 1024)
    tf = 256 if d_ff % 256 == 0 else 128
    tf_b = 512 if d_ff % 512 == 0 else 128
    tn_ada = 1024 if (N_ADA * d) % 1024 == 0 else 128
    tn_in = math.gcd(n_main, 768)
    gd_hp = 2 if gd_heads % 2 == 0 else 1
    gd_unroll = 4
    na_rq = 4

    w_out_b = w_out.astype(BF16)
    half = HEAD_DIM // 2
    w_in_t = jnp.swapaxes(w_in, 1, 2)
    w_tail_t = jnp.zeros((depth, HEAD_DIM, d), F32)
    w_tail_t = w_tail_t.at[:, 0:ndh].set(w_in_t[:, n_main:n_main + ndh])
    w_tail_t = w_tail_t.at[:, half:half + ndh].set(w_in_t[:, n_main + ndh:n_main + 2 * ndh])
    alog_t = jnp.zeros((depth, HEAD_DIM, GD_CHUNK), F32).at[:, half:half + ndh, :].set(
        jnp.broadcast_to(gd_a_log.reshape(depth, ndh, 1), (depth, ndh, GD_CHUNK)))
    dtb_t = jnp.zeros((depth, HEAD_DIM, GD_CHUNK), F32).at[:, half:half + ndh, :].set(
        jnp.broadcast_to(gd_dt_bias.reshape(depth, ndh, 1), (depth, ndh, GD_CHUNK)))
    ln_g4 = ln_g.reshape(depth, 3, 1, d)
    ln_b4 = ln_b.reshape(depth, 3, 1, d)
    norm_w3 = gd_norm_w.reshape(depth, 1, HEAD_DIM)
    cos_x, sin_x = _rope_tables(t)
    cos_c, sin_c = jnp.ones((HEAD_DIM, lc), F32), jnp.zeros((HEAD_DIM, lc), F32)

    cvec = jnp.zeros((8, d), F32).at[0:b].set(c).at[b].set(c_ctx)
    ada = _ada_table(cvec, w_ada, b_ada, tn_ada).reshape(depth, 8, N_ADA, 1, d)

    grp_x = lambda i: (i * tm) // t
    grp_x_big = lambda i: (i * tm_big) // t
    grp_c = lambda i: b

    def ffn_pair(which, xv, cv, l, with_ctx):
        w_gu, w_dn = ((ffn1_w_gu, ffn1_w_down), (ffn2_w_gu, ffn2_w_down))[which]
        j0, k_ln = (0, 0) if which == 0 else (6, 2)
        f32_w = (w_gu, w_gu, d_ff, w_dn, l)
        if not with_ctx:
            return _ffn(xv, ada, ln_g4, ln_b4, l, j0, k_ln, grp_x_big, f32_w, tm_big, tf, alpha), cv
        cv, wg_b, wu_b, wd_b = _ffn(cv, ada, ln_g4, ln_b4, l, j0, k_ln, grp_c, f32_w, tm_big, tf, alpha, emit=True)
        xv = _ffn(xv, ada, ln_g4, ln_b4, l, j0, k_ln, grp_x, (wg_b, wu_b, 0, wd_b, None), tm, tf_b, alpha, ahead=True)
        return xv, cv

    xs = x.reshape(b * t, d)
    cs = ctx.reshape(b * lc, d)
    cv_blk = 3 * na_heads
    gd_blk = cv_blk + 3 * cv_groups
    z_blk = gd_blk + 3 * gd_heads
    na_rows = t // GRID_W
    assert na_rows % na_rq == 0 and na_rows >= win_r + na_rq and (na_rq * GRID_W) % HEAD_DIM == 0
    na_kinds, na_starts = _na_plan(na_rows, min(win_r, na_rows), win_r, na_rq)
    bias = _na_bias_table(na_rpb, na_kinds, na_rows, win_r, win_c, na_rq)

    for l in range(depth):
        last = l == depth - 1
        xs, cs = ffn_pair(0, xs, cs, l, True)

        p_x, tl_x = _inproj(xs, ada, l, grp_x_big, w_in_t, w_tail_t, n_main, tn_in, tm_big)
        p_c, tl_c = _inproj(cs, ada, l, grp_c, w_in_t, w_tail_t, n_main, tn_in, tm_big)

        o_na_x = _na_attention(p_x, p_c, bias, na_starts, l, b, t, lc, na_heads, win_r, na_rq)
        o_cv_x = _short_conv(p_x, cv_conv_w, l, b, t, cv_groups, cv_blk)

        rows_x, dt_x = _gdn_gates(tl_x, alog_t[l], dtb_t[l], gd_heads)
        rows_c, dt_c = _gdn_gates(tl_c, alog_t[l], dtb_t[l], gd_heads)
        gt_x = _gdn_prep(p_x, gd_conv_w, cos_x, sin_x, l, b, t, gd_blk, gd_heads)
        gt_c = _gdn_prep(p_c, gd_conv_w, cos_c, sin_c, l, b, lc, gd_blk, gd_heads)
        o_gd_x, o_gd_c = _gdn_scan(gt_x, gt_c, rows_x, rows_c, dt_x, dt_c, p_x, p_c, norm_w3, l,
                                   b, t, lc, gd_heads, z_blk, gd_hp, gd_unroll)

        xs = _outproj(xs, o_na_x, o_cv_x, o_gd_x, w_out_b, ada, ln_g4, ln_b4, l, grp_x, tm, alpha)
        if not last:
            o_na_c = _dense_attention(p_c, b, lc, na_heads)
            o_cv_c = _short_conv(p_c, cv_conv_w, l, b, lc, cv_groups, cv_blk)
            cs = _outproj(cs, o_na_c, o_cv_c, o_gd_c, w_out_b, ada, ln_g4, ln_b4, l, grp_c, tm, alpha)
        xs, cs = ffn_pair(1, xs, cs, l, not last)
    return xs.reshape(b, t, d)
```

```python
import functools
import math

import jax
import jax.numpy as jnp
from jax import lax
from jax.experimental import pallas as pl
from jax.experimental.pallas import tpu as pltpu

F32 = jnp.float32
BF16 = jnp.bfloat16

HEAD_DIM = 128
GRID_W = 64
GD_CHUNK = 128
ROPE_BASE = 10000.0
LN_EPS = 1e-6
NEG_INF = -1e30
N_ADA = 9
V7X_VMEM_LIMIT = 56 * 1024 * 1024


def _cparams(sem):
    return pltpu.CompilerParams(dimension_semantics=sem, vmem_limit_bytes=V7X_VMEM_LIMIT)


def _ln(x):
    mu = jnp.mean(x, axis=-1, keepdims=True)
    xc = x - mu
    var = jnp.mean(xc * xc, axis=-1, keepdims=True)
    return xc * lax.rsqrt(var + LN_EPS)


def _silu(x):
    return x * jax.nn.sigmoid(x)


def _softplus(x):
    return jnp.maximum(x, 0.0) + jnp.log(1.0 + jnp.exp(-jnp.abs(x)))


def _dot(a, b):
    return jnp.dot(a, b, preferred_element_type=F32)


def _dot_nt(a, b):
    return lax.dot_general(a, b, (((1,), (1,)), ((), ())), preferred_element_type=F32)


def _dot_tn(a, b):
    return lax.dot_general(a, b, (((0,), (0,)), ((), ())), preferred_element_type=F32)


def _ada_kernel(c_ref, w_ref, b_ref, o_ref):
    s = _silu(c_ref[...]).astype(BF16)
    o_ref[...] = _dot(s, w_ref[...].astype(BF16)) + b_ref[...]


def _ada_table(cvec, w_ada, b_ada, tn):
    depth, d, n = w_ada.shape
    return pl.pallas_call(
        _ada_kernel,
        grid=(depth, n // tn),
        in_specs=[pl.BlockSpec((8, d), lambda l, j: (0, 0)),
                  pl.BlockSpec((None, d, tn), lambda l, j: (l, 0, j)),
                  pl.BlockSpec((None, 1, tn), lambda l, j: (l, 0, j))],
        out_specs=pl.BlockSpec((None, 8, tn), lambda l, j: (l, 0, j)),
        out_shape=jax.ShapeDtypeStruct((depth, 8, n), F32),
        compiler_params=_cparams(("parallel", "parallel")),
        name="ada_table",
    )(cvec, w_ada, b_ada.reshape(depth, 1, n))


def _ada_spec(d, l, grp, j):
    return pl.BlockSpec((None, None, None, 1, d), lambda i, *_: (l, grp(i), j, 0, 0))


def _ln_spec(d, l, k):
    return pl.BlockSpec((None, None, 1, d), lambda *_: (l, k, 0, 0))


def _ffn_kernel(*refs, alpha, emit, ahead, slab):
    refs = list(refs)
    x_ref, sh_ref, sc_ref, gt_ref = refs[:4]
    del refs[:4]
    if ahead:
        xn_ref, shn_ref, scn_ref = refs[:3]
        del refs[:3]
    wg_ref, wu_ref, wd_ref, lg_ref, lb_ref, o_ref = refs[:6]
    del refs[:6]
    if emit:
        wgb_ref, wub_ref, wdb_ref = refs[:3]
        del refs[:3]
    h_refs = refs
    i, f = pl.program_id(0), pl.program_id(1)
    tm = x_ref.shape[0]

    def modulate(xv, sc, sh):
        return (_ln(xv) * (1.0 + sc) + sh).astype(BF16)

    @pl.when(f == 0)
    def _():
        o_ref[...] = jnp.zeros_like(o_ref)

    @pl.when((f == 0) & (i == 0) if ahead else f == 0)
    def _():
        h_refs[0][...] = modulate(x_ref[...], sc_ref[...], sh_ref[...])

    def step(h_cur, h_nxt):
        wg, wu, wd = wg_ref[...].astype(BF16), wu_ref[...].astype(BF16), wd_ref[...].astype(BF16)
        if emit:
            wgb_ref[...] = wg
            wub_ref[...] = wu
            wdb_ref[...] = wd
        h = h_cur[...]
        a = (_silu(_dot(h, wg)) * _dot(h, wu)).astype(BF16)
        o_ref[...] += _dot(a, wd)
        if h_nxt is not None:
            rows = pl.ds(pl.multiple_of(jnp.minimum(f * slab, tm - slab), 16), slab)
            h_nxt[rows, :] = modulate(xn_ref[rows, :], scn_ref[...], shn_ref[...])

    if ahead:
        pl.when(i % 2 == 0)(lambda: step(h_refs[0], h_refs[1]))
        pl.when(i % 2 == 1)(lambda: step(h_refs[1], h_refs[0]))
    else:
        step(h_refs[0], None)

    @pl.when(f == pl.num_programs(1) - 1)
    def _():
        z = alpha * x_ref[...] + (0.5 * gt_ref[...]) * o_ref[...]
        o_ref[...] = _ln(z) * lg_ref[...] + lb_ref[...]


def _ffn(x, ada, ln_g, ln_b, l, j0, k_ln, grp, weights, tm, tf, alpha, emit_tf=None, ahead=False):
    m, d = x.shape
    emit = emit_tf is not None
    if weights[0] == "f32":
        _, w_gu, w_down = weights
        d_ff = w_down.shape[1]
        nf = d_ff // tf
        w_specs = [pl.BlockSpec((None, d, tf), lambda i, f: (l, 0, f)),
                   pl.BlockSpec((None, d, tf), lambda i, f: (l, 0, f + nf)),
                   pl.BlockSpec((None, tf, d), lambda i, f: (l, f, 0))]
        w_args = [w_gu, w_gu, w_down]
    else:
        _, w_gate, w_up, w_down = weights
        d_ff = w_down.shape[0]
        nf = d_ff // tf
        assert w_gate.shape == (nf, d, tf)
        w_specs = [pl.BlockSpec((None, d, tf), lambda i, f: (f, 0, 0)),
                   pl.BlockSpec((None, d, tf), lambda i, f: (f, 0, 0)),
                   pl.BlockSpec((tf, d), lambda i, f: (f, 0))]
        w_args = [w_gate, w_up, w_down]
    nt = m // tm
    nxt = lambda i: jnp.minimum(i + 1, nt - 1)
    slab = -(-pl.cdiv(tm, nf) // 16) * 16
    assert (not ahead) or (slab * nf >= tm and tm % 16 == 0 and slab <= tm)
    single = weights[0] == "f32"
    x_spec = (pl.BlockSpec((tm, d), lambda i, f: (i, 0), pipeline_mode=pl.Buffered(1)) if single
              else pl.BlockSpec((tm, d), lambda i, f: (i, 0)))
    in_specs = [x_spec, _ada_spec(d, l, grp, j0), _ada_spec(d, l, grp, j0 + 1), _ada_spec(d, l, grp, j0 + 2)]
    args = [x, ada, ada, ada]
    if ahead:
        grp_n = lambda i: grp(nxt(i))
        in_specs += [pl.BlockSpec((tm, d), lambda i, f: (nxt(i), 0)),
                     _ada_spec(d, l, grp_n, j0), _ada_spec(d, l, grp_n, j0 + 1)]
        args += [x, ada, ada]
    in_specs += w_specs + [_ln_spec(d, l, k_ln), _ln_spec(d, l, k_ln)]
    args += w_args + [ln_g, ln_b]
    out_specs = [pl.BlockSpec((tm, d), lambda i, f: (i, 0))]
    out_shape = [jax.ShapeDtypeStruct((m, d), F32)]
    if emit:
        r = emit_tf // tf
        assert emit_tf % tf == 0 and d_ff % emit_tf == 0
        tile_major = pl.BlockSpec((None, d, tf), lambda i, f: (f // r, 0, f % r))
        out_specs += [tile_major, tile_major, pl.BlockSpec((tf, d), lambda i, f: (f, 0))]
        out_shape += [jax.ShapeDtypeStruct((d_ff // emit_tf, d, emit_tf), BF16)] * 2 + [
            jax.ShapeDtypeStruct((d_ff, d), BF16)]
    res = pl.pallas_call(
        functools.partial(_ffn_kernel, alpha=alpha, emit=emit, ahead=ahead, slab=slab),
        grid=(nt, nf),
        in_specs=in_specs,
        out_specs=out_specs,
        out_shape=out_shape,
        scratch_shapes=[pltpu.VMEM((tm, d), BF16)] * (2 if ahead else 1),
        compiler_params=_cparams(("arbitrary", "arbitrary")),
        name="ffn",
    )(*args)
    return res if emit else res[0]


def _inproj_kernel(x_ref, sh_ref, sc_ref, w_ref, wt_ref, o_ref, t_ref, h_ref):
    @pl.when(pl.program_id(1) == 0)
    def _():
        h = _ln(x_ref[...]) * (1.0 + sc_ref[...]) + sh_ref[...]
        h_ref[...] = h.astype(BF16)
        t_ref[...] = _dot_nt(h_ref[...], wt_ref[...].astype(BF16))

    o_ref[...] = _dot_nt(h_ref[...], w_ref[...].astype(BF16)).astype(o_ref.dtype)


def _inproj(x, ada, l, grp, w_in_t, w_tail_t, n_main, tn, tm):
    m, d = x.shape
    return pl.pallas_call(
        _inproj_kernel,
        grid=(m // tm, n_main // tn),
        in_specs=[pl.BlockSpec((tm, d), lambda i, n: (i, 0)),
                  _ada_spec(d, l, grp, 3), _ada_spec(d, l, grp, 4),
                  pl.BlockSpec((None, tn, d), lambda i, n: (l, n, 0)),
                  pl.BlockSpec((None, HEAD_DIM, d), lambda i, n: (l, 0, 0))],
        out_specs=[pl.BlockSpec((tm, tn), lambda i, n: (i, n)),
                   pl.BlockSpec((tm, HEAD_DIM), lambda i, n: (i, 0))],
        out_shape=[jax.ShapeDtypeStruct((m, n_main), BF16),
                   jax.ShapeDtypeStruct((m, HEAD_DIM), F32)],
        scratch_shapes=[pltpu.VMEM((tm, d), BF16)],
        compiler_params=_cparams(("parallel", "arbitrary")),
        name="inproj",
    )(x, ada, ada, w_in_t, w_tail_t)


LOG2_E = 1.4426950408889634


def _softmax_pv(s_parts, v_parts, scale):
    m = functools.reduce(jnp.maximum, [jnp.max(s, axis=-1, keepdims=True) for s in s_parts])
    ps = [jnp.exp2((s - m) * (scale * LOG2_E)) for s in s_parts]
    den = functools.reduce(lambda a, b: a + b, [jnp.sum(p, axis=-1, keepdims=True) for p in ps])
    o = functools.reduce(lambda a, b: a + b, [_dot(p.astype(BF16), v) for p, v in zip(ps, v_parts)])
    return o / den


def _na_plan(rows, wr, win_r, rq):
    band = lambda r: min(max(r - wr // 2, 0), rows - wr)
    kinds, starts = [], []
    for rb in range(rows // rq):
        r0 = rb * rq
        u0 = min(band(r0), rows - (wr + rq))
        kind = tuple((band(r0 + i) - (r0 + i) + win_r - 1, band(r0 + i) - u0) for i in range(rq))
        if not kinds or kinds[-1] != kind:
            kinds.append(kind)
            starts.append(rb)
    return kinds, starts


def _na_kernel(q_ref, k_ref, v_ref, kc_ref, vc_ref, b_ref, o_ref, *, heads, rows, wr, rq, scale):
    r0 = pl.program_id(1) * rq
    nbr = wr + rq
    u0 = jnp.minimum(jnp.clip(r0 - wr // 2, 0, rows - wr), rows - nbr)
    start = pl.multiple_of(u0 * GRID_W, GRID_W)
    nb = nbr * GRID_W

    def scores(h):
        cs = slice(h * HEAD_DIM, (h + 1) * HEAD_DIM)
        q = q_ref[:, cs]
        return [_dot_nt(q, k_ref[pl.ds(start, nb), cs]) + b_ref[h], _dot_nt(q, kc_ref[:, cs])]

    pending = scores(0)
    for h in range(heads):
        nxt = scores(h + 1) if h + 1 < heads else None
        cs = slice(h * HEAD_DIM, (h + 1) * HEAD_DIM)
        o = _softmax_pv(pending, [v_ref[pl.ds(start, nb), cs], vc_ref[:, cs]], scale)
        o_ref[:, cs] = o.astype(o_ref.dtype)
        pending = nxt


def _na_attention(px, pc, bias, starts, l, b, t, lc, heads, win_r, rq):
    na_d = heads * HEAD_DIM
    rows = t // GRID_W
    wr = min(win_r, rows)
    nrb = rows // rq
    nb = (wr + rq) * GRID_W

    def kind(r):
        return sum((r >= s0).astype(jnp.int32) for s0 in starts[1:]) if len(starts) > 1 else 0

    return pl.pallas_call(
        functools.partial(_na_kernel, heads=heads, rows=rows, wr=wr, rq=rq, scale=HEAD_DIM ** -0.5),
        grid=(b, nrb),
        in_specs=[pl.BlockSpec((rq * GRID_W, na_d), lambda i, r: (i * nrb + r, 0)),
                  pl.BlockSpec((t, na_d), lambda i, r: (i, 1)),
                  pl.BlockSpec((t, na_d), lambda i, r: (i, 2)),
                  pl.BlockSpec((lc, na_d), lambda i, r: (i, 1)),
                  pl.BlockSpec((lc, na_d), lambda i, r: (i, 2)),
                  pl.BlockSpec((None, heads, None, rq * GRID_W, nb), lambda i, r: (l, 0, kind(r), 0, 0))],
        out_specs=pl.BlockSpec((rq * GRID_W, na_d), lambda i, r: (i * nrb + r, 0)),
        out_shape=jax.ShapeDtypeStruct((b * t, na_d), BF16),
        compiler_params=_cparams(("parallel", "arbitrary")),
        name="na_attention",
    )(px, px, px, pc, pc, bias)


def _dense_attn_kernel(q_ref, k_ref, v_ref, o_ref, *, heads, scale):
    for h in range(heads):
        cs = slice(h * HEAD_DIM, (h + 1) * HEAD_DIM)
        s = _dot_nt(q_ref[:, cs], k_ref[:, cs])
        o_ref[:, cs] = _softmax_pv([s], [v_ref[:, cs]], scale).astype(o_ref.dtype)


def _dense_attention(pc, b, lc, heads):
    na_d = heads * HEAD_DIM
    return pl.pallas_call(
        functools.partial(_dense_attn_kernel, heads=heads, scale=HEAD_DIM ** -0.5),
        grid=(b,),
        in_specs=[pl.BlockSpec((lc, na_d), lambda i: (i, 0)),
                  pl.BlockSpec((lc, na_d), lambda i: (i, 1)),
                  pl.BlockSpec((lc, na_d), lambda i: (i, 2))],
        out_specs=pl.BlockSpec((lc, na_d), lambda i: (i, 0)),
        out_shape=jax.ShapeDtypeStruct((b * lc, na_d), BF16),
        compiler_params=_cparams(("parallel",)),
        name="ctx_attention",
    )(pc, pc, pc)


def _na_bias_kernel(r_ref, o_ref, *, win_r, win_c, wr, kinds, inv_scale):
    n_dr, n_dc = 2 * win_r - 1, 2 * win_c - 1
    base = (pl.program_id(0) * pl.num_programs(1) + pl.program_id(1)) * (n_dr * n_dc)
    q = lax.broadcasted_iota(jnp.int32, (GRID_W, GRID_W), 0)
    k = lax.broadcasted_iota(jnp.int32, (GRID_W, GRID_W), 1)
    dc = jnp.clip(k - q + win_c - 1, 0, n_dc - 1)
    col_start = jnp.clip(q - win_c // 2, 0, GRID_W - win_c)
    col_ok = (k >= col_start) & (k < col_start + win_c)
    o_ref[...] = jnp.full(o_ref.shape, NEG_INF, F32)
    for a in range(n_dr):
        acc = jnp.zeros((GRID_W, GRID_W), F32)
        for j in range(n_dc):
            acc = jnp.where(dc == j, r_ref[base + a * n_dc + j], acc)
        g = jnp.where(col_ok, acc * inv_scale, NEG_INF)
        for kd, kind in enumerate(kinds):
            for i, (dr0, off) in enumerate(kind):
                w = a - dr0
                if 0 <= w < wr:
                    o_ref[kd, i * GRID_W:(i + 1) * GRID_W, (off + w) * GRID_W:(off + w + 1) * GRID_W] = g


def _na_bias_table(rpb, kinds, rows, win_r, win_c, rq):
    depth, heads = rpb.shape[0], rpb.shape[1]
    wr = min(win_r, rows)
    shape = (len(kinds), rq * GRID_W, (wr + rq) * GRID_W)
    return pl.pallas_call(
        functools.partial(_na_bias_kernel, win_r=win_r, win_c=win_c, wr=wr, kinds=kinds, inv_scale=HEAD_DIM ** 0.5),
        grid=(depth, heads),
        in_specs=[pl.BlockSpec(memory_space=pltpu.SMEM)],
        out_specs=pl.BlockSpec((None, None) + shape, lambda l, h: (l, h, 0, 0, 0)),
        out_shape=jax.ShapeDtypeStruct((depth, heads) + shape, F32),
        compiler_params=_cparams(("parallel", "parallel")),
        name="na_bias",
    )(rpb.reshape(-1))


SUBLANES = 8


def _dwconv_rows(x, w, pad_ref):
    ksz, t = w.shape[0], x.shape[0]
    assert ksz // 2 <= SUBLANES
    halo = jnp.zeros((SUBLANES, x.shape[1]), F32)
    pad_ref[0:SUBLANES, :] = halo
    pad_ref[SUBLANES + t:2 * SUBLANES + t, :] = halo
    pad_ref[SUBLANES:SUBLANES + t, :] = x
    y = None
    for j in range(ksz):
        start = SUBLANES + j - ksz // 2
        term = w[j:j + 1, :] * pad_ref[start:start + t, :]
        y = term if y is None else y + term
    return y


def _cv_kernel(b_ref, c_ref, u_ref, w_ref, o_ref, pad_ref):
    cu = c_ref[...].astype(F32) * u_ref[...].astype(F32)
    o_ref[...] = (b_ref[...].astype(F32) * _dwconv_rows(cu, w_ref[...], pad_ref)).astype(o_ref.dtype)


def _short_conv(p, cv_w, l, nseq, tseq, cv_groups, col0_blk):
    ksz = cv_w.shape[1]
    cv_d = cv_groups * HEAD_DIM
    return pl.pallas_call(
        _cv_kernel,
        grid=(nseq, cv_groups),
        in_specs=[pl.BlockSpec((tseq, HEAD_DIM), lambda s, j: (s, col0_blk + j)),
                  pl.BlockSpec((tseq, HEAD_DIM), lambda s, j: (s, col0_blk + cv_groups + j)),
                  pl.BlockSpec((tseq, HEAD_DIM), lambda s, j: (s, col0_blk + 2 * cv_groups + j)),
                  pl.BlockSpec((None, ksz, HEAD_DIM), lambda s, j: (l, 0, j))],
        out_specs=pl.BlockSpec((tseq, HEAD_DIM), lambda s, j: (s, j)),
        out_shape=jax.ShapeDtypeStruct((nseq * tseq, cv_d), BF16),
        scratch_shapes=[pltpu.VMEM((tseq + 2 * SUBLANES, HEAD_DIM), F32)],
        compiler_params=_cparams(("parallel", "parallel")),
        name="short_conv",
    )(p, p, p, cv_w)


def _gdn_prep_kernel(*refs, gd_heads, cw):
    x_refs = refs[:cw]
    w_ref, cos_ref, sin_ref, o_ref, pad_ref = refs[cw:]
    j0 = pl.program_id(1) * cw

    def silu_conv_t(c):
        w = w_ref[:, c * HEAD_DIM:(c + 1) * HEAD_DIM]
        return _silu(_dwconv_rows(x_refs[c][...].astype(F32), w, pad_ref)).T

    def emit(c, res):
        for k in range(o_ref.shape[1]):
            o_ref[c, k] = res[:, k * GD_CHUNK:(k + 1) * GD_CHUNK]

    @pl.when(j0 < 2 * gd_heads)
    def _():
        for c in range(cw):
            st = silu_conv_t(c)
            ss = jnp.sum(st * st, axis=0, keepdims=True)
            nrm = st * (lax.rsqrt(ss + LN_EPS) * jnp.where(j0 < gd_heads, HEAD_DIM ** -0.5, 1.0))
            q4 = HEAD_DIM // 4
            swapped = jnp.concatenate([nrm[q4:2 * q4], nrm[0:q4], nrm[3 * q4:4 * q4], nrm[2 * q4:3 * q4]], axis=0)
            emit(c, nrm * cos_ref[...] + swapped * sin_ref[...])

    @pl.when(j0 >= 2 * gd_heads)
    def _():
        for c in range(cw):
            emit(c, silu_conv_t(c))


def _gdn_prep(p, conv_w, cos_t, sin_t, l, nseq, tseq, col0_blk, gd_heads, cw):
    ksz = conv_w.shape[1]
    nch = tseq // GD_CHUNK
    assert gd_heads % cw == 0
    x_specs = [pl.BlockSpec((tseq, HEAD_DIM), functools.partial(lambda s, j, c: (s, col0_blk + j * cw + c), c=c))
               for c in range(cw)]
    return pl.pallas_call(
        functools.partial(_gdn_prep_kernel, gd_heads=gd_heads, cw=cw),
        grid=(nseq, 3 * gd_heads // cw),
        in_specs=x_specs + [pl.BlockSpec((None, ksz, cw * HEAD_DIM), lambda s, j: (l, 0, j)),
                            pl.BlockSpec((HEAD_DIM, tseq), lambda s, j: (0, 0)),
                            pl.BlockSpec((HEAD_DIM, tseq), lambda s, j: (0, 0))],
        out_specs=pl.BlockSpec((cw, nch, HEAD_DIM, GD_CHUNK), lambda s, j: (j, s, 0, 0)),
        out_shape=jax.ShapeDtypeStruct((3 * gd_heads, nseq * nch, HEAD_DIM, GD_CHUNK), F32),
        scratch_shapes=[pltpu.VMEM((tseq + 2 * SUBLANES, HEAD_DIM), F32)],
        compiler_params=_cparams(("parallel", "parallel")),
        name="gdn_prep",
    )(*([p] * cw), conv_w, cos_t, sin_t)


def _gdn_gate_kernel(t_ref, alog_ref, dtb_ref, rows_ref, dt_ref, *, gd_heads):
    for k in range(rows_ref.shape[0]):
        _gdn_gate_chunk(t_ref.at[k * GD_CHUNK:(k + 1) * GD_CHUNK, :], alog_ref, dtb_ref, rows_ref.at[k], dt_ref.at[k],
                        gd_heads)


def _gdn_gate_chunk(t_ref, alog_ref, dtb_ref, rows_ref, dt_ref, gd_heads):
    ndh = 2 * gd_heads
    half = HEAD_DIM // 2
    a = t_ref[...].T
    beta = jax.nn.sigmoid(a)
    g = -jnp.exp(alog_ref[...]) * _softplus(a + dtb_ref[...])
    ri = lax.broadcasted_iota(jnp.int32, (GD_CHUNK, GD_CHUNK), 0)
    ci = lax.broadcasted_iota(jnp.int32, (GD_CHUNK, GD_CHUNK), 1)
    g1 = g.astype(BF16)
    r1 = g - g1.astype(F32)
    g2 = r1.astype(BF16)
    g3 = (r1 - g2.astype(F32)).astype(BF16)

    def cumsum(tri):
        tb = jnp.where(tri, 1.0, 0.0).astype(BF16)
        return _dot(g1, tb) + _dot(g2, tb) + _dot(g3, tb)

    gc = jnp.where(ri < half + gd_heads, cumsum(ri <= ci), cumsum(ri >= ci))
    glast = jnp.sum(g, axis=1, keepdims=True)
    e = jnp.exp(gc)
    f = jnp.exp(glast - gc)
    gt = jnp.exp(glast) + jnp.zeros_like(gc)
    gct = gc.T
    zero_rows = jnp.zeros((3, GD_CHUNK), F32)
    for dh in range(ndh):
        r = half + dh
        rows_ref[dh, 0:1, :] = beta[dh:dh + 1]
        rows_ref[dh, 1:2, :] = beta[dh:dh + 1] * e[r:r + 1]
        rows_ref[dh, 2:3, :] = f[r:r + 1]
        rows_ref[dh, 3:4, :] = gt[r:r + 1]
        rows_ref[dh, 4:5, :] = e[r:r + 1]
        rows_ref[dh, 5:8, :] = zero_rows
        col = jnp.sum(jnp.where(ci == r, gct, 0.0), axis=1, keepdims=True)
        diff = gc[r:r + 1] - col
        mask = (ci >= ri) if dh < gd_heads else (ci <= ri)
        dt_ref[dh] = jnp.where(mask, jnp.exp(jnp.where(mask, diff, 0.0)), 0.0)


def _gdn_gates(tail, alog_t, dtb_t, gd_heads):
    m = tail.shape[0]
    nch = m // GD_CHUNK
    ndh = 2 * gd_heads
    per = 4 if nch % 4 == 0 else 1
    return pl.pallas_call(
        functools.partial(_gdn_gate_kernel, gd_heads=gd_heads),
        grid=(nch // per,),
        in_specs=[pl.BlockSpec((per * GD_CHUNK, HEAD_DIM), lambda n: (n, 0)),
                  pl.BlockSpec((HEAD_DIM, GD_CHUNK), lambda n: (0, 0)),
                  pl.BlockSpec((HEAD_DIM, GD_CHUNK), lambda n: (0, 0))],
        out_specs=[pl.BlockSpec((per, ndh, 8, GD_CHUNK), lambda n: (n, 0, 0, 0)),
                   pl.BlockSpec((per, ndh, GD_CHUNK, GD_CHUNK), lambda n: (n, 0, 0, 0))],
        out_shape=[jax.ShapeDtypeStruct((nch, ndh, 8, GD_CHUNK), F32),
                   jax.ShapeDtypeStruct((nch, ndh, GD_CHUNK, GD_CHUNK), F32)],
        compiler_params=_cparams(("parallel",)),
        name="gdn_gates",
    )(tail, alog_t, dtb_t)


def _tri_inverse_multi(mnegs, ri, ci):
    eye = jnp.where(ri == ci, 1.0, 0.0)

    def blk(n):
        return (ri // n) == (ci // n)

    b16 = blk(16)
    qs = [jnp.where(b16, m, 0.0) for m in mnegs]
    xs = [eye + q for q in qs]
    qbs = [q.astype(BF16) for q in qs]
    qs = [_dot(qb, qb) for qb in qbs]
    for _ in range(2):
        qbs = [q.astype(BF16) for q in qs]
        xq = [_dot(x.astype(BF16), qb) for x, qb in zip(xs, qbs)]
        qs = [_dot(qb, qb) for qb in qbs]
        xs = [x + d for x, d in zip(xs, xq)]
    xq = [_dot(x.astype(BF16), q.astype(BF16)) for x, q in zip(xs, qs)]
    xs = [x + d for x, d in zip(xs, xq)]
    for n in (32, 64, 128):
        msk = blk(n) & jnp.logical_not(blk(n // 2))
        offs = [jnp.where(msk, m, 0.0).astype(BF16) for m in mnegs]
        xbs = [x.astype(BF16) for x in xs]
        t1 = [_dot(xb, off).astype(BF16) for xb, off in zip(xbs, offs)]
        t2 = [_dot(t, xb) for t, xb in zip(t1, xbs)]
        xs = [x + d for x, d in zip(xs, t2)]
    return xs


def _gdn_kernel(*refs, ncc, ncx, hp, unroll):
    (qx, kx, vx, qc, kc, vc, rx0, rx1, rc0, rc1, dx0, dx1, dc0, dc1) = refs[:14]
    zx = refs[14:14 + hp]
    zc = refs[14 + hp:14 + 2 * hp]
    nw, ox, oc, ut_s, wq_s, at_s, kf_s, o_s = refs[14 + 2 * hp:]
    ri = lax.broadcasted_iota(jnp.int32, (GD_CHUNK, GD_CHUNK), 0)
    ci = lax.broadcasted_iota(jnp.int32, (GD_CHUNK, GD_CHUNK), 1)
    strict = (ci > ri, ci < ri)

    def local(units, base, q_ref, k_ref, v_ref, rows, dts):
        kts = [k_ref[p, i] for p, i in units]
        kbs = [kt.astype(BF16) for kt in kts]
        kks = [_dot_tn(kb, kb) for kb in kbs]
        qks = [_dot_tn(kb, q_ref[p, i].astype(BF16)) for kb, (p, i) in zip(kbs, units)]
        chains = [(u, d) for u in range(len(units)) for d in range(2)]
        mnegs = []
        for u, d in chains:
            p, i = units[u]
            mnegs.append(-(jnp.where(strict[d], dts[d][i, p], 0.0) * kks[u] * rows[d][i, p][0:1]))
        tts = _tri_inverse_multi(mnegs, ri, ci)
        sols = []
        for (u, d), tt in zip(chains, tts):
            p, i = units[u]
            rw = rows[d][i, p]
            rhs = jnp.concatenate([v_ref[p, i] * rw[0:1], kts[u] * rw[1:2]], axis=0).astype(BF16)
            sols.append(_dot(rhs, tt.astype(BF16)))
        for (u, d), sol in zip(chains, sols):
            p, i = units[u]
            n = base + i
            rw = rows[d][i, p]
            ut_s[p, d, n] = sol[:HEAD_DIM]
            wq_s[p, d, n, :, 0:GD_CHUNK] = sol[HEAD_DIM:].astype(BF16)
            wq_s[p, d, n, :, GD_CHUNK:2 * GD_CHUNK] = (q_ref[p, i] * rw[4:5]).astype(BF16)
            at_s[p, d, n] = (qks[u] * dts[d][i, p]).astype(BF16)
            kf_s[p, d, n] = (kts[u] * rw[2:3]).astype(BF16)
        for p, i in units:
            o_s[p, base + i] = jnp.zeros((HEAD_DIM, GD_CHUNK), F32)

    def seg_local(q_ref, k_ref, v_ref, rows, dts, count, base):
        per = min(unroll, count)

        def body(it, carry):
            units = [(p, it * per + u) for u in range(per) for p in range(hp)]
            local(units, base, q_ref, k_ref, v_ref, rows, dts)
            return carry
        lax.fori_loop(0, count // per, body, 0)

    seg_local(qc, kc, vc, (rc0, rc1), (dc0, dc1), ncc, 0)
    seg_local(qx, kx, vx, (rx0, rx1), (dx0, dx1), ncx, ncc)

    def seg_scan(states, rows, count, base):
        def body(i, carry):
            j = count - 1 - i
            chains = [(p, d, (i, j)[d]) for p in range(hp) for d in range(2)]
            ys = [_dot(st.astype(BF16), wq_s[p, d, base + c]) for st, (p, d, c) in zip(carry, chains)]
            vnbs = [(ut_s[p, d, base + c] - y[:, 0:GD_CHUNK]).astype(BF16) for y, (p, d, c) in zip(ys, chains)]
            ots = [_dot(vnb, at_s[p, d, base + c]) for vnb, (p, d, c) in zip(vnbs, chains)]
            upd = [_dot_nt(vnb, kf_s[p, d, base + c]) for vnb, (p, d, c) in zip(vnbs, chains)]
            for y, ot, (p, d, c) in zip(ys, ots, chains):
                o_s[p, base + c] += y[:, GD_CHUNK:2 * GD_CHUNK] + ot
            return tuple(st * rows[d][c, p][3:4] + du for st, du, (p, d, c) in zip(carry, upd, chains))
        return lax.fori_loop(0, count, body, states)

    zero = jnp.zeros((HEAD_DIM, HEAD_DIM), F32)
    states = seg_scan((zero,) * (2 * hp), (rc0, rc1), ncc, 0)
    seg_scan(states, (rx0, rx1), ncx, ncc)

    def seg_out(o_ref, z_refs, count, base):
        def body(i, carry):
            rs = pl.ds(pl.multiple_of(i * GD_CHUNK, GD_CHUNK), GD_CHUNK)
            for p in range(hp):
                ot = o_s[p, base + i]
                ms = jnp.mean(ot * ot, axis=0, keepdims=True)
                on = (ot * lax.rsqrt(ms + LN_EPS)).T
                z = z_refs[p][rs, :].astype(F32)
                o_ref[rs, p * HEAD_DIM:(p + 1) * HEAD_DIM] = (on * nw[...] * _silu(z)).astype(o_ref.dtype)
            return carry
        lax.fori_loop(0, count, body, 0)

    seg_out(oc, zc, ncc, 0)
    seg_out(ox, zx, ncx, ncc)


def _gdn_scan(gt_x, gt_c, rows_x, rows_c, dt_x, dt_c, px, pc, norm_w, l, b, t, lc, gd_heads, z_blk, hp, unroll):
    ncx, ncc = t // GD_CHUNK, lc // GD_CHUNK
    nc = ncx + ncc
    gd_d = gd_heads * HEAD_DIM
    nhb = gd_heads // hp

    def tile_spec(n, which):
        return pl.BlockSpec((hp, n, HEAD_DIM, GD_CHUNK), lambda i, h: (which * nhb + h, i, 0, 0))

    def row_spec(n, d):
        return pl.BlockSpec((n, hp, 8, GD_CHUNK), lambda i, h: (i, d * nhb + h, 0, 0))

    def dt_spec(n, d):
        return pl.BlockSpec((n, hp, GD_CHUNK, GD_CHUNK), lambda i, h: (i, d * nhb + h, 0, 0))

    def z_spec(n, p):
        return pl.BlockSpec((n, HEAD_DIM), lambda i, h: (i, z_blk + h * hp + p))

    return pl.pallas_call(
        functools.partial(_gdn_kernel, ncc=ncc, ncx=ncx, hp=hp, unroll=unroll),
        grid=(b, nhb),
        in_specs=[tile_spec(ncx, 0), tile_spec(ncx, 1), tile_spec(ncx, 2),
                  tile_spec(ncc, 0), tile_spec(ncc, 1), tile_spec(ncc, 2),
                  row_spec(ncx, 0), row_spec(ncx, 1), row_spec(ncc, 0), row_spec(ncc, 1),
                  dt_spec(ncx, 0), dt_spec(ncx, 1), dt_spec(ncc, 0), dt_spec(ncc, 1)]
                 + [z_spec(t, p) for p in range(hp)] + [z_spec(lc, p) for p in range(hp)]
                 + [pl.BlockSpec((None, 1, HEAD_DIM), lambda i, h: (l, 0, 0))],
        out_specs=[pl.BlockSpec((t, hp * HEAD_DIM), lambda i, h: (i, h)),
                   pl.BlockSpec((lc, hp * HEAD_DIM), lambda i, h: (i, h))],
        out_shape=[jax.ShapeDtypeStruct((b * t, gd_d), BF16),
                   jax.ShapeDtypeStruct((b * lc, gd_d), BF16)],
        scratch_shapes=[pltpu.VMEM((hp, 2, nc, HEAD_DIM, GD_CHUNK), F32),
                        pltpu.VMEM((hp, 2, nc, HEAD_DIM, 2 * GD_CHUNK), BF16),
                        pltpu.VMEM((hp, 2, nc, GD_CHUNK, GD_CHUNK), BF16),
                        pltpu.VMEM((hp, 2, nc, HEAD_DIM, GD_CHUNK), BF16),
                        pltpu.VMEM((hp, nc, HEAD_DIM, GD_CHUNK), F32)],
        compiler_params=_cparams(("parallel", "arbitrary")),
        name="gdn_scan",
    )(gt_x, gt_x, gt_x, gt_c, gt_c, gt_c, rows_x, rows_x, rows_c, rows_c, dt_x, dt_x, dt_c, dt_c,
      *([px] * hp), *([pc] * hp), norm_w)


def _outproj_kernel(x_ref, na_ref, cv_ref, gd_ref, w_ref, gt_ref, lg_ref, lb_ref, o_ref, *, alpha):
    n0 = na_ref.shape[1]
    n1 = n0 + cv_ref.shape[1]
    tm = x_ref.shape[0]
    nsub = 2 if tm % 32 == 0 else 1
    for k in range(nsub):
        rs = slice(k * tm // nsub, (k + 1) * tm // nsub)
        y = (_dot(na_ref[rs, :], w_ref[0:n0, :]) + _dot(cv_ref[rs, :], w_ref[n0:n1, :])
             + _dot(gd_ref[rs, :], w_ref[n1:, :]))
        z = alpha * x_ref[rs, :] + gt_ref[...] * y
        o_ref[rs, :] = _ln(z) * lg_ref[...] + lb_ref[...]


def _outproj(x, o_na, o_cv, o_gd, w_out, ada, ln_g, ln_b, l, grp, tm, alpha):
    m, d = x.shape
    d_mix = w_out.shape[1]
    return pl.pallas_call(
        functools.partial(_outproj_kernel, alpha=alpha),
        grid=(m // tm,),
        in_specs=[pl.BlockSpec((tm, d), lambda i: (i, 0)),
                  pl.BlockSpec((tm, o_na.shape[1]), lambda i: (i, 0)),
                  pl.BlockSpec((tm, o_cv.shape[1]), lambda i: (i, 0)),
                  pl.BlockSpec((tm, o_gd.shape[1]), lambda i: (i, 0)),
                  pl.BlockSpec((None, d_mix, d), lambda i: (l, 0, 0)),
                  _ada_spec(d, l, grp, 5), _ln_spec(d, l, 1), _ln_spec(d, l, 1)],
        out_specs=pl.BlockSpec((tm, d), lambda i: (i, 0)),
        out_shape=jax.ShapeDtypeStruct((m, d), F32),
        compiler_params=_cparams(("parallel",)),
        name="outproj",
    )(x, o_na, o_cv, o_gd, w_out, ada, ln_g, ln_b)


def _rope_tables(t):
    tok = jnp.arange(t, dtype=jnp.int32)
    row = (tok // GRID_W).astype(F32)
    col = (tok % GRID_W).astype(F32)
    n_freq = HEAD_DIM // 4
    inv_freq = ROPE_BASE ** (-jnp.arange(n_freq, dtype=F32) / n_freq)
    ar = (row[:, None] * inv_freq).T
    ac = (col[:, None] * inv_freq).T
    cos_t = jnp.concatenate([jnp.cos(ar), jnp.cos(ar), jnp.cos(ac), jnp.cos(ac)], axis=0)
    sin_t = jnp.concatenate([-jnp.sin(ar), jnp.sin(ar), -jnp.sin(ac), jnp.sin(ac)], axis=0)
    return cos_t, sin_t


def kernel(x, c, ctx, c_ctx, w_ada, b_ada, ln_g, ln_b, ffn1_w_gu, ffn1_w_down, w_in, na_rpb, cv_conv_w,
           gd_conv_w, gd_a_log, gd_dt_bias, gd_norm_w, w_out, ffn2_w_gu, ffn2_w_down):
    b, t, d = x.shape
    lc = ctx.shape[1]
    depth = w_ada.shape[0]
    d_ff = ffn1_w_down.shape[1]
    na_heads = na_rpb.shape[1]
    win_r, win_c = (na_rpb.shape[2] + 1) // 2, (na_rpb.shape[3] + 1) // 2
    cv_d = cv_conv_w.shape[2]
    cv_groups = cv_d // HEAD_DIM
    gd_heads = gd_a_log.shape[2]
    na_d, gd_d = na_heads * HEAD_DIM, gd_heads * HEAD_DIM
    ndh = 2 * gd_heads
    n_main = 3 * na_d + 3 * cv_d + 3 * gd_d + gd_d
    alpha = (2 * depth) ** 0.25
    assert b + 1 <= 8 and t % GD_CHUNK == 0 and lc % GD_CHUNK == 0 and ndh <= HEAD_DIM // 2
    assert t // GRID_W >= win_r

    tm = math.gcd(math.gcd(t, b * lc), 512)
    tm_big = math.gcd(math.gcd(t, b * lc), 1024)
    tf = 256 if d_ff % 256 == 0 else 128
    tf_b = 512 if d_ff % 512 == 0 else 128
    tn_ada = 1024 if (N_ADA * d) % 1024 == 0 else 128
    tn_in = math.gcd(n_main, 768)
    gd_hp = 2 if gd_heads % 2 == 0 else 1
    gd_unroll = 4
    na_rq = 4

    w_out_b = w_out.astype(BF16)
    half = HEAD_DIM // 2
    w_in_t = jnp.swapaxes(w_in, 1, 2)
    w_tail_t = jnp.zeros((depth, HEAD_DIM, d), F32)
    w_tail_t = w_tail_t.at[:, 0:ndh].set(w_in_t[:, n_main:n_main + ndh])
    w_tail_t = w_tail_t.at[:, half:half + ndh].set(w_in_t[:, n_main + ndh:n_main + 2 * ndh])
    alog_t = jnp.zeros((depth, HEAD_DIM, GD_CHUNK), F32).at[:, half:half + ndh, :].set(
        jnp.broadcast_to(gd_a_log.reshape(depth, ndh, 1), (depth, ndh, GD_CHUNK)))
    dtb_t = jnp.zeros((depth, HEAD_DIM, GD_CHUNK), F32).at[:, half:half + ndh, :].set(
        jnp.broadcast_to(gd_dt_bias.reshape(depth, ndh, 1), (depth, ndh, GD_CHUNK)))
    ln_g4 = ln_g.reshape(depth, 3, 1, d)
    ln_b4 = ln_b.reshape(depth, 3, 1, d)
    norm_w3 = gd_norm_w.reshape(depth, 1, HEAD_DIM)
    cos_x, sin_x = _rope_tables(t)
    cos_c, sin_c = jnp.ones((HEAD_DIM, lc), F32), jnp.zeros((HEAD_DIM, lc), F32)

    cvec = jnp.zeros((8, d), F32).at[0:b].set(c).at[b].set(c_ctx)
    ada = _ada_table(cvec, w_ada, b_ada, tn_ada).reshape(depth, 8, N_ADA, 1, d)

    grp_x = lambda i: (i * tm) // t
    grp_x_big = lambda i: (i * tm_big) // t
    grp_c = lambda i: b

    def ffn_pair(which, xv, cv, l, with_ctx):
        w_gu, w_dn = ((ffn1_w_gu, ffn1_w_down), (ffn2_w_gu, ffn2_w_down))[which]
        j0, k_ln = (0, 0) if which == 0 else (6, 2)
        f32_w = ("f32", w_gu, w_dn)
        if not with_ctx:
            return _ffn(xv, ada, ln_g4, ln_b4, l, j0, k_ln, grp_x_big, f32_w, tm_big, tf, alpha), cv
        cv, wg_b, wu_b, wd_b = _ffn(cv, ada, ln_g4, ln_b4, l, j0, k_ln, grp_c, f32_w, tm_big, tf, alpha, emit_tf=tf_b)
        xv = _ffn(xv, ada, ln_g4, ln_b4, l, j0, k_ln, grp_x, ("bf16", wg_b, wu_b, wd_b), tm, tf_b, alpha, ahead=True)
        return xv, cv

    xs = x.reshape(b * t, d)
    cs = ctx.reshape(b * lc, d)
    cv_blk = 3 * na_heads
    gd_blk = cv_blk + 3 * cv_groups
    z_blk = gd_blk + 3 * gd_heads
    na_rows = t // GRID_W
    assert na_rows % na_rq == 0 and na_rows >= win_r + na_rq and (na_rq * GRID_W) % HEAD_DIM == 0
    na_kinds, na_starts = _na_plan(na_rows, min(win_r, na_rows), win_r, na_rq)
    bias = _na_bias_table(na_rpb, na_kinds, na_rows, win_r, win_c, na_rq)

    for l in range(depth):
        last = l == depth - 1
        xs, cs = ffn_pair(0, xs, cs, l, True)

        p_x, tl_x = _inproj(xs, ada, l, grp_x_big, w_in_t, w_tail_t, n_main, tn_in, tm_big)
        p_c, tl_c = _inproj(cs, ada, l, grp_c, w_in_t, w_tail_t, n_main, tn_in, tm_big)

        o_na_x = _na_attention(p_x, p_c, bias, na_starts, l, b, t, lc, na_heads, win_r, na_rq)
        o_cv_x = _short_conv(p_x, cv_conv_w, l, b, t, cv_groups, cv_blk)

        rows_x, dt_x = _gdn_gates(tl_x, alog_t[l], dtb_t[l], gd_heads)
        rows_c, dt_c = _gdn_gates(tl_c, alog_t[l], dtb_t[l], gd_heads)
        gt_x = _gdn_prep(p_x, gd_conv_w, cos_x, sin_x, l, b, t, gd_blk, gd_heads, 1)
        gt_c = _gdn_prep(p_c, gd_conv_w, cos_c, sin_c, l, b, lc, gd_blk, gd_heads, gd_heads)
        o_gd_x, o_gd_c = _gdn_scan(gt_x, gt_c, rows_x, rows_c, dt_x, dt_c, p_x, p_c, norm_w3, l,
                                   b, t, lc, gd_heads, z_blk, gd_hp, gd_unroll)

        xs = _outproj(xs, o_na_x, o_cv_x, o_gd_x, w_out_b, ada, ln_g4, ln_b4, l, grp_x, tm, alpha)
        if not last:
            o_na_c = _dense_attention(p_c, b, lc, na_heads)
            o_cv_c = _short_conv(p_c, cv_conv_w, l, b, lc, cv_groups, cv_blk)
            cs = _outproj(cs, o_na_c, o_cv_c, o_gd_c, w_out_b, ada, ln_g4, ln_b4, l, grp_c, tm, alpha)
        xs, cs = ffn_pair(1, xs, cs, l, not last)
    return xs.reshape(b, t, d)
```

```python
import functools
import math

import jax
import jax.numpy as jnp
from jax import lax
from jax.experimental import pallas as pl
from jax.experimental.pallas import tpu as pltpu

F32 = jnp.float32
BF16 = jnp.bfloat16

HEAD_DIM = 128
GRID_W = 64
GD_CHUNK = 128
ROPE_BASE = 10000.0
LN_EPS = 1e-6
NEG_INF = -1e30
N_ADA = 9
V7X_VMEM_LIMIT = 56 * 1024 * 1024


def _cparams(sem):
    return pltpu.CompilerParams(dimension_semantics=sem, vmem_limit_bytes=V7X_VMEM_LIMIT)


def _ln(x):
    mu = jnp.mean(x, axis=-1, keepdims=True)
    xc = x - mu
    var = jnp.mean(xc * xc, axis=-1, keepdims=True)
    return xc * lax.rsqrt(var + LN_EPS)


def _silu(x):
    return x * jax.nn.sigmoid(x)


def _softplus(x):
    return jnp.maximum(x, 0.0) + jnp.log(1.0 + jnp.exp(-jnp.abs(x)))


def _dot(a, b):
    return jnp.dot(a, b, preferred_element_type=F32)


def _dot_nt(a, b):
    return lax.dot_general(a, b, (((1,), (1,)), ((), ())), preferred_element_type=F32)


def _dot_tn(a, b):
    return lax.dot_general(a, b, (((0,), (0,)), ((), ())), preferred_element_type=F32)


def _ada_kernel(c_ref, w_ref, b_ref, o_ref):
    s = _silu(c_ref[...]).astype(BF16)
    o_ref[...] = _dot(s, w_ref[...].astype(BF16)) + b_ref[...]


def _ada_table(cvec, w_ada, b_ada, tn):
    depth, d, n = w_ada.shape
    return pl.pallas_call(
        _ada_kernel,
        grid=(depth, n // tn),
        in_specs=[pl.BlockSpec((8, d), lambda l, j: (0, 0)),
                  pl.BlockSpec((None, d, tn), lambda l, j: (l, 0, j)),
                  pl.BlockSpec((None, 1, tn), lambda l, j: (l, 0, j))],
        out_specs=pl.BlockSpec((None, 8, tn), lambda l, j: (l, 0, j)),
        out_shape=jax.ShapeDtypeStruct((depth, 8, n), F32),
        compiler_params=_cparams(("parallel", "parallel")),
        name="ada_table",
    )(cvec, w_ada, b_ada.reshape(depth, 1, n))


def _ada_spec(d, l, grp, j):
    return pl.BlockSpec((None, None, None, 1, d), lambda i, *_: (l, grp(i), j, 0, 0))


def _ln_spec(d, l, k):
    return pl.BlockSpec((None, None, 1, d), lambda *_: (l, k, 0, 0))


def _ffn_kernel(*refs, alpha, emit, ahead, slab):
    refs = list(refs)
    x_ref, sh_ref, sc_ref, gt_ref = refs[:4]
    del refs[:4]
    if ahead:
        xn_ref, shn_ref, scn_ref = refs[:3]
        del refs[:3]
    wg_ref, wu_ref, wd_ref, lg_ref, lb_ref, o_ref = refs[:6]
    del refs[:6]
    if emit:
        wgb_ref, wub_ref, wdb_ref = refs[:3]
        del refs[:3]
    h_refs = refs
    i, f = pl.program_id(0), pl.program_id(1)
    tm = x_ref.shape[0]

    def modulate(xv, sc, sh):
        return (_ln(xv) * (1.0 + sc) + sh).astype(BF16)

    @pl.when(f == 0)
    def _():
        o_ref[...] = jnp.zeros_like(o_ref)

    @pl.when((f == 0) & (i == 0) if ahead else f == 0)
    def _():
        h_refs[0][...] = modulate(x_ref[...], sc_ref[...], sh_ref[...])

    def step(h_cur, h_nxt):
        wg, wu, wd = wg_ref[...].astype(BF16), wu_ref[...].astype(BF16), wd_ref[...].astype(BF16)
        if emit:
            wgb_ref[...] = wg
            wub_ref[...] = wu
            wdb_ref[...] = wd
        h = h_cur[...]
        a = (_silu(_dot(h, wg)) * _dot(h, wu)).astype(BF16)
        o_ref[...] += _dot(a, wd)
        if h_nxt is not None:
            rows = pl.ds(pl.multiple_of(jnp.minimum(f * slab, tm - slab), 16), slab)
            h_nxt[rows, :] = modulate(xn_ref[rows, :], scn_ref[...], shn_ref[...])

    if ahead:
        pl.when(i % 2 == 0)(lambda: step(h_refs[0], h_refs[1]))
        pl.when(i % 2 == 1)(lambda: step(h_refs[1], h_refs[0]))
    else:
        step(h_refs[0], None)

    @pl.when(f == pl.num_programs(1) - 1)
    def _():
        z = alpha * x_ref[...] + (0.5 * gt_ref[...]) * o_ref[...]
        o_ref[...] = _ln(z) * lg_ref[...] + lb_ref[...]


def _ffn(x, ada, ln_g, ln_b, l, j0, k_ln, grp, weights, tm, tf, alpha, emit_tf=None, ahead=False):
    m, d = x.shape
    emit = emit_tf is not None
    if weights[0] == "f32":
        _, w_gu, w_down = weights
        d_ff = w_down.shape[1]
        nf = d_ff // tf
        w_specs = [pl.BlockSpec((None, d, tf), lambda i, f: (l, 0, f)),
                   pl.BlockSpec((None, d, tf), lambda i, f: (l, 0, f + nf)),
                   pl.BlockSpec((None, tf, d), lambda i, f: (l, f, 0))]
        w_args = [w_gu, w_gu, w_down]
    else:
        _, w_gate, w_up, w_down = weights
        d_ff = w_down.shape[0]
        nf = d_ff // tf
        assert w_gate.shape == (nf, d, tf)
        w_specs = [pl.BlockSpec((None, d, tf), lambda i, f: (f, 0, 0)),
                   pl.BlockSpec((None, d, tf), lambda i, f: (f, 0, 0)),
                   pl.BlockSpec((tf, d), lambda i, f: (f, 0))]
        w_args = [w_gate, w_up, w_down]
    nt = m // tm
    nxt = lambda i: jnp.minimum(i + 1, nt - 1)
    slab = -(-pl.cdiv(tm, nf) // 16) * 16
    assert (not ahead) or (slab * nf >= tm and tm % 16 == 0 and slab <= tm)
    single = weights[0] == "f32"
    x_spec = (pl.BlockSpec((tm, d), lambda i, f: (i, 0), pipeline_mode=pl.Buffered(1)) if single
              else pl.BlockSpec((tm, d), lambda i, f: (i, 0)))
    in_specs = [x_spec, _ada_spec(d, l, grp, j0), _ada_spec(d, l, grp, j0 + 1), _ada_spec(d, l, grp, j0 + 2)]
    args = [x, ada, ada, ada]
    if ahead:
        grp_n = lambda i: grp(nxt(i))
        in_specs += [pl.BlockSpec((tm, d), lambda i, f: (nxt(i), 0)),
                     _ada_spec(d, l, grp_n, j0), _ada_spec(d, l, grp_n, j0 + 1)]
        args += [x, ada, ada]
    in_specs += w_specs + [_ln_spec(d, l, k_ln), _ln_spec(d, l, k_ln)]
    args += w_args + [ln_g, ln_b]
    out_specs = [pl.BlockSpec((tm, d), lambda i, f: (i, 0))]
    out_shape = [jax.ShapeDtypeStruct((m, d), F32)]
    if emit:
        r = emit_tf // tf
        assert emit_tf % tf == 0 and d_ff % emit_tf == 0
        tile_major = pl.BlockSpec((None, d, tf), lambda i, f: (f // r, 0, f % r))
        out_specs += [tile_major, tile_major, pl.BlockSpec((tf, d), lambda i, f: (f, 0))]
        out_shape += [jax.ShapeDtypeStruct((d_ff // emit_tf, d, emit_tf), BF16)] * 2 + [
            jax.ShapeDtypeStruct((d_ff, d), BF16)]
    res = pl.pallas_call(
        functools.partial(_ffn_kernel, alpha=alpha, emit=emit, ahead=ahead, slab=slab),
        grid=(nt, nf),
        in_specs=in_specs,
        out_specs=out_specs,
        out_shape=out_shape,
        scratch_shapes=[pltpu.VMEM((tm, d), BF16)] * (2 if ahead else 1),
        compiler_params=_cparams(("arbitrary", "arbitrary")),
        name="ffn",
    )(*args)
    return res if emit else res[0]


def _inproj_kernel(x_ref, sh_ref, sc_ref, w_ref, wt_ref, o_ref, t_ref, h_ref):
    @pl.when(pl.program_id(1) == 0)
    def _():
        h = _ln(x_ref[...]) * (1.0 + sc_ref[...]) + sh_ref[...]
        h_ref[...] = h.astype(BF16)
        t_ref[...] = _dot_nt(h_ref[...], wt_ref[...].astype(BF16))

    o_ref[...] = _dot_nt(h_ref[...], w_ref[...].astype(BF16)).astype(o_ref.dtype)


def _inproj(x, ada, l, grp, w_in_t, w_tail_t, n_main, tn, tm):
    m, d = x.shape
    return pl.pallas_call(
        _inproj_kernel,
        grid=(m // tm, n_main // tn),
        in_specs=[pl.BlockSpec((tm, d), lambda i, n: (i, 0)),
                  _ada_spec(d, l, grp, 3), _ada_spec(d, l, grp, 4),
                  pl.BlockSpec((None, tn, d), lambda i, n: (l, n, 0)),
                  pl.BlockSpec((None, HEAD_DIM, d), lambda i, n: (l, 0, 0))],
        out_specs=[pl.BlockSpec((tm, tn), lambda i, n: (i, n)),
                   pl.BlockSpec((tm, HEAD_DIM), lambda i, n: (i, 0))],
        out_shape=[jax.ShapeDtypeStruct((m, n_main), BF16),
                   jax.ShapeDtypeStruct((m, HEAD_DIM), F32)],
        scratch_shapes=[pltpu.VMEM((tm, d), BF16)],
        compiler_params=_cparams(("parallel", "arbitrary")),
        name="inproj",
    )(x, ada, ada, w_in_t, w_tail_t)


LOG2_E = 1.4426950408889634


def _softmax_pv(s_parts, v_parts, scale):
    m = functools.reduce(jnp.maximum, [jnp.max(s, axis=-1, keepdims=True) for s in s_parts])
    ps = [jnp.exp2((s - m) * (scale * LOG2_E)) for s in s_parts]
    den = functools.reduce(lambda a, b: a + b, [jnp.sum(p, axis=-1, keepdims=True) for p in ps])
    o = functools.reduce(lambda a, b: a + b, [_dot(p.astype(BF16), v) for p, v in zip(ps, v_parts)])
    return o / den


def _na_plan(rows, wr, win_r, rq):
    band = lambda r: min(max(r - wr // 2, 0), rows - wr)
    kinds, starts = [], []
    for rb in range(rows // rq):
        r0 = rb * rq
        u0 = min(band(r0), rows - (wr + rq))
        kind = tuple((band(r0 + i) - (r0 + i) + win_r - 1, band(r0 + i) - u0) for i in range(rq))
        if not kinds or kinds[-1] != kind:
            kinds.append(kind)
            starts.append(rb)
    return kinds, starts


def _na_kernel(q_ref, k_ref, v_ref, kc_ref, vc_ref, b_ref, o_ref, *, heads, rows, wr, rq, scale):
    r0 = pl.program_id(1) * rq
    nbr = wr + rq
    u0 = jnp.minimum(jnp.clip(r0 - wr // 2, 0, rows - wr), rows - nbr)
    start = pl.multiple_of(u0 * GRID_W, GRID_W)
    nb = nbr * GRID_W

    def scores(h):
        cs = slice(h * HEAD_DIM, (h + 1) * HEAD_DIM)
        q = q_ref[:, cs]
        return [_dot_nt(q, k_ref[pl.ds(start, nb), cs]) + b_ref[h], _dot_nt(q, kc_ref[:, cs])]

    pending = scores(0)
    for h in range(heads):
        nxt = scores(h + 1) if h + 1 < heads else None
        cs = slice(h * HEAD_DIM, (h + 1) * HEAD_DIM)
        o = _softmax_pv(pending, [v_ref[pl.ds(start, nb), cs], vc_ref[:, cs]], scale)
        o_ref[:, cs] = o.astype(o_ref.dtype)
        pending = nxt


def _na_attention(px, pc, bias, starts, l, b, t, lc, heads, win_r, rq):
    na_d = heads * HEAD_DIM
    rows = t // GRID_W
    wr = min(win_r, rows)
    nrb = rows // rq
    nb = (wr + rq) * GRID_W

    def kind(r):
        return sum((r >= s0).astype(jnp.int32) for s0 in starts[1:]) if len(starts) > 1 else 0

    return pl.pallas_call(
        functools.partial(_na_kernel, heads=heads, rows=rows, wr=wr, rq=rq, scale=HEAD_DIM ** -0.5),
        grid=(b, nrb),
        in_specs=[pl.BlockSpec((rq * GRID_W, na_d), lambda i, r: (i * nrb + r, 0)),
                  pl.BlockSpec((t, na_d), lambda i, r: (i, 1)),
                  pl.BlockSpec((t, na_d), lambda i, r: (i, 2)),
                  pl.BlockSpec((lc, na_d), lambda i, r: (i, 1)),
                  pl.BlockSpec((lc, na_d), lambda i, r: (i, 2)),
                  pl.BlockSpec((None, heads, None, rq * GRID_W, nb), lambda i, r: (l, 0, kind(r), 0, 0))],
        out_specs=pl.BlockSpec((rq * GRID_W, na_d), lambda i, r: (i * nrb + r, 0)),
        out_shape=jax.ShapeDtypeStruct((b * t, na_d), BF16),
        compiler_params=_cparams(("parallel", "arbitrary")),
        name="na_attention",
    )(px, px, px, pc, pc, bias)


def _dense_attn_kernel(q_ref, k_ref, v_ref, o_ref, *, heads, scale):
    for h in range(heads):
        cs = slice(h * HEAD_DIM, (h + 1) * HEAD_DIM)
        s = _dot_nt(q_ref[:, cs], k_ref[:, cs])
        o_ref[:, cs] = _softmax_pv([s], [v_ref[:, cs]], scale).astype(o_ref.dtype)


def _dense_attention(pc, b, lc, heads):
    na_d = heads * HEAD_DIM
    return pl.pallas_call(
        functools.partial(_dense_attn_kernel, heads=heads, scale=HEAD_DIM ** -0.5),
        grid=(b,),
        in_specs=[pl.BlockSpec((lc, na_d), lambda i: (i, 0)),
                  pl.BlockSpec((lc, na_d), lambda i: (i, 1)),
                  pl.BlockSpec((lc, na_d), lambda i: (i, 2))],
        out_specs=pl.BlockSpec((lc, na_d), lambda i: (i, 0)),
        out_shape=jax.ShapeDtypeStruct((b * lc, na_d), BF16),
        compiler_params=_cparams(("parallel",)),
        name="ctx_attention",
    )(pc, pc, pc)


def _na_bias_kernel(r_ref, o_ref, *, win_r, win_c, wr, kinds, inv_scale):
    n_dr, n_dc = 2 * win_r - 1, 2 * win_c - 1
    base = (pl.program_id(0) * pl.num_programs(1) + pl.program_id(1)) * (n_dr * n_dc)
    q = lax.broadcasted_iota(jnp.int32, (GRID_W, GRID_W), 0)
    k = lax.broadcasted_iota(jnp.int32, (GRID_W, GRID_W), 1)
    dc = jnp.clip(k - q + win_c - 1, 0, n_dc - 1)
    col_start = jnp.clip(q - win_c // 2, 0, GRID_W - win_c)
    col_ok = (k >= col_start) & (k < col_start + win_c)
    o_ref[...] = jnp.full(o_ref.shape, NEG_INF, F32)
    for a in range(n_dr):
        acc = jnp.zeros((GRID_W, GRID_W), F32)
        for j in range(n_dc):
            acc = jnp.where(dc == j, r_ref[base + a * n_dc + j], acc)
        g = jnp.where(col_ok, acc * inv_scale, NEG_INF)
        for kd, kind in enumerate(kinds):
            for i, (dr0, off) in enumerate(kind):
                w = a - dr0
                if 0 <= w < wr:
                    o_ref[kd, i * GRID_W:(i + 1) * GRID_W, (off + w) * GRID_W:(off + w + 1) * GRID_W] = g


def _na_bias_table(rpb, kinds, rows, win_r, win_c, rq):
    depth, heads = rpb.shape[0], rpb.shape[1]
    wr = min(win_r, rows)
    shape = (len(kinds), rq * GRID_W, (wr + rq) * GRID_W)
    return pl.pallas_call(
        functools.partial(_na_bias_kernel, win_r=win_r, win_c=win_c, wr=wr, kinds=kinds, inv_scale=HEAD_DIM ** 0.5),
        grid=(depth, heads),
        in_specs=[pl.BlockSpec(memory_space=pltpu.SMEM)],
        out_specs=pl.BlockSpec((None, None) + shape, lambda l, h: (l, h, 0, 0, 0)),
        out_shape=jax.ShapeDtypeStruct((depth, heads) + shape, F32),
        compiler_params=_cparams(("parallel", "parallel")),
        name="na_bias",
    )(rpb.reshape(-1))


SUBLANES = 8


def _dwconv_rows(x, w, pad_ref):
    ksz, t = w.shape[0], x.shape[0]
    assert ksz // 2 <= SUBLANES
    halo = jnp.zeros((SUBLANES, x.shape[1]), F32)
    pad_ref[0:SUBLANES, :] = halo
    pad_ref[SUBLANES + t:2 * SUBLANES + t, :] = halo
    pad_ref[SUBLANES:SUBLANES + t, :] = x
    y = None
    for j in range(ksz):
        start = SUBLANES + j - ksz // 2
        term = w[j:j + 1, :] * pad_ref[start:start + t, :]
        y = term if y is None else y + term
    return y


def _cv_kernel(b_ref, c_ref, u_ref, w_ref, o_ref, pad_ref):
    cu = c_ref[...].astype(F32) * u_ref[...].astype(F32)
    o_ref[...] = (b_ref[...].astype(F32) * _dwconv_rows(cu, w_ref[...], pad_ref)).astype(o_ref.dtype)


def _short_conv(p, cv_w, l, nseq, tseq, cv_groups, col0_blk):
    ksz = cv_w.shape[1]
    cv_d = cv_groups * HEAD_DIM
    return pl.pallas_call(
        _cv_kernel,
        grid=(nseq, cv_groups),
        in_specs=[pl.BlockSpec((tseq, HEAD_DIM), lambda s, j: (s, col0_blk + j)),
                  pl.BlockSpec((tseq, HEAD_DIM), lambda s, j: (s, col0_blk + cv_groups + j)),
                  pl.BlockSpec((tseq, HEAD_DIM), lambda s, j: (s, col0_blk + 2 * cv_groups + j)),
                  pl.BlockSpec((None, ksz, HEAD_DIM), lambda s, j: (l, 0, j))],
        out_specs=pl.BlockSpec((tseq, HEAD_DIM), lambda s, j: (s, j)),
        out_shape=jax.ShapeDtypeStruct((nseq * tseq, cv_d), BF16),
        scratch_shapes=[pltpu.VMEM((tseq + 2 * SUBLANES, HEAD_DIM), F32)],
        compiler_params=_cparams(("parallel", "parallel")),
        name="short_conv",
    )(p, p, p, cv_w)


def _gdn_prep_kernel(*refs, gd_heads, cw):
    x_refs = refs[:cw]
    w_ref, cos_ref, sin_ref, o_ref, pad_ref = refs[cw:]
    j0 = pl.program_id(1) * cw

    def silu_conv_t(c):
        w = w_ref[:, c * HEAD_DIM:(c + 1) * HEAD_DIM]
        return _silu(_dwconv_rows(x_refs[c][...].astype(F32), w, pad_ref)).T

    def emit(c, res):
        for k in range(o_ref.shape[1]):
            o_ref[c, k] = res[:, k * GD_CHUNK:(k + 1) * GD_CHUNK]

    @pl.when(j0 < 2 * gd_heads)
    def _():
        for c in range(cw):
            st = silu_conv_t(c)
            ss = jnp.sum(st * st, axis=0, keepdims=True)
            nrm = st * (lax.rsqrt(ss + LN_EPS) * jnp.where(j0 < gd_heads, HEAD_DIM ** -0.5, 1.0))
            q4 = HEAD_DIM // 4
            swapped = jnp.concatenate([nrm[q4:2 * q4], nrm[0:q4], nrm[3 * q4:4 * q4], nrm[2 * q4:3 * q4]], axis=0)
            emit(c, nrm * cos_ref[...] + swapped * sin_ref[...])

    @pl.when(j0 >= 2 * gd_heads)
    def _():
        for c in range(cw):
            emit(c, silu_conv_t(c))


def _gdn_prep(p, conv_w, cos_t, sin_t, l, nseq, tseq, col0_blk, gd_heads, cw):
    ksz = conv_w.shape[1]
    nch = tseq // GD_CHUNK
    assert gd_heads % cw == 0
    x_specs = [pl.BlockSpec((tseq, HEAD_DIM), functools.partial(lambda s, j, c: (s, col0_blk + j * cw + c), c=c))
               for c in range(cw)]
    return pl.pallas_call(
        functools.partial(_gdn_prep_kernel, gd_heads=gd_heads, cw=cw),
        grid=(nseq, 3 * gd_heads // cw),
        in_specs=x_specs + [pl.BlockSpec((None, ksz, cw * HEAD_DIM), lambda s, j: (l, 0, j)),
                            pl.BlockSpec((HEAD_DIM, tseq), lambda s, j: (0, 0)),
                            pl.BlockSpec((HEAD_DIM, tseq), lambda s, j: (0, 0))],
        out_specs=pl.BlockSpec((cw, nch, HEAD_DIM, GD_CHUNK), lambda s, j: (j, s, 0, 0)),
        out_shape=jax.ShapeDtypeStruct((3 * gd_heads, nseq * nch, HEAD_DIM, GD_CHUNK), F32),
        scratch_shapes=[pltpu.VMEM((tseq + 2 * SUBLANES, HEAD_DIM), F32)],
        compiler_params=_cparams(("parallel", "parallel")),
        name="gdn_prep",
    )(*([p] * cw), conv_w, cos_t, sin_t)


def _gdn_gate_kernel(t_ref, alog_ref, dtb_ref, rows_ref, dt_ref, *, gd_heads):
    for k in range(rows_ref.shape[0]):
        _gdn_gate_chunk(t_ref.at[k * GD_CHUNK:(k + 1) * GD_CHUNK, :], alog_ref, dtb_ref, rows_ref.at[k], dt_ref.at[k],
                        gd_heads)


def _gdn_gate_chunk(t_ref, alog_ref, dtb_ref, rows_ref, dt_ref, gd_heads):
    ndh = 2 * gd_heads
    half = HEAD_DIM // 2
    a = t_ref[...].T
    beta = jax.nn.sigmoid(a)
    g = -jnp.exp(alog_ref[...]) * _softplus(a + dtb_ref[...])
    ri = lax.broadcasted_iota(jnp.int32, (GD_CHUNK, GD_CHUNK), 0)
    ci = lax.broadcasted_iota(jnp.int32, (GD_CHUNK, GD_CHUNK), 1)
    g1 = g.astype(BF16)
    r1 = g - g1.astype(F32)
    g2 = r1.astype(BF16)
    g3 = (r1 - g2.astype(F32)).astype(BF16)

    def cumsum(tri):
        tb = jnp.where(tri, 1.0, 0.0).astype(BF16)
        return _dot(g1, tb) + _dot(g2, tb) + _dot(g3, tb)

    gc = jnp.where(ri < half + gd_heads, cumsum(ri <= ci), cumsum(ri >= ci))
    glast = jnp.sum(g, axis=1, keepdims=True)
    e = jnp.exp(gc)
    f = jnp.exp(glast - gc)
    gt = jnp.exp(glast) + jnp.zeros_like(gc)
    gct = gc.T
    zero_rows = jnp.zeros((3, GD_CHUNK), F32)
    for dh in range(ndh):
        r = half + dh
        rows_ref[dh, 0:1, :] = beta[dh:dh + 1]
        rows_ref[dh, 1:2, :] = beta[dh:dh + 1] * e[r:r + 1]
        rows_ref[dh, 2:3, :] = f[r:r + 1]
        rows_ref[dh, 3:4, :] = gt[r:r + 1]
        rows_ref[dh, 4:5, :] = e[r:r + 1]
        rows_ref[dh, 5:8, :] = zero_rows
        col = jnp.sum(jnp.where(ci == r, gct, 0.0), axis=1, keepdims=True)
        diff = gc[r:r + 1] - col
        mask = (ci >= ri) if dh < gd_heads else (ci <= ri)
        dt_ref[dh] = jnp.where(mask, jnp.exp(jnp.where(mask, diff, 0.0)), 0.0)


def _gdn_gates(tail, alog_t, dtb_t, gd_heads):
    m = tail.shape[0]
    nch = m // GD_CHUNK
    ndh = 2 * gd_heads
    per = 8 if nch % 8 == 0 else 1
    return pl.pallas_call(
        functools.partial(_gdn_gate_kernel, gd_heads=gd_heads),
        grid=(nch // per,),
        in_specs=[pl.BlockSpec((per * GD_CHUNK, HEAD_DIM), lambda n: (n, 0)),
                  pl.BlockSpec((HEAD_DIM, GD_CHUNK), lambda n: (0, 0)),
                  pl.BlockSpec((HEAD_DIM, GD_CHUNK), lambda n: (0, 0))],
        out_specs=[pl.BlockSpec((per, ndh, 8, GD_CHUNK), lambda n: (n, 0, 0, 0)),
                   pl.BlockSpec((per, ndh, GD_CHUNK, GD_CHUNK), lambda n: (n, 0, 0, 0))],
        out_shape=[jax.ShapeDtypeStruct((nch, ndh, 8, GD_CHUNK), F32),
                   jax.ShapeDtypeStruct((nch, ndh, GD_CHUNK, GD_CHUNK), F32)],
        compiler_params=_cparams(("parallel",)),
        name="gdn_gates",
    )(tail, alog_t, dtb_t)


def _tri_inverse_multi(mnegs, ri, ci):
    eye = jnp.where(ri == ci, 1.0, 0.0)

    def blk(n):
        return (ri // n) == (ci // n)

    b16 = blk(16)
    qs = [jnp.where(b16, m, 0.0) for m in mnegs]
    xs = [eye + q for q in qs]
    qbs = [q.astype(BF16) for q in qs]
    qs = [_dot(qb, qb) for qb in qbs]
    for _ in range(2):
        qbs = [q.astype(BF16) for q in qs]
        xq = [_dot(x.astype(BF16), qb) for x, qb in zip(xs, qbs)]
        qs = [_dot(qb, qb) for qb in qbs]
        xs = [x + d for x, d in zip(xs, xq)]
    xq = [_dot(x.astype(BF16), q.astype(BF16)) for x, q in zip(xs, qs)]
    xs = [x + d for x, d in zip(xs, xq)]
    for n in (32, 64, 128):
        msk = blk(n) & jnp.logical_not(blk(n // 2))
        offs = [jnp.where(msk, m, 0.0).astype(BF16) for m in mnegs]
        xbs = [x.astype(BF16) for x in xs]
        t1 = [_dot(xb, off).astype(BF16) for xb, off in zip(xbs, offs)]
        t2 = [_dot(t, xb) for t, xb in zip(t1, xbs)]
        xs = [x + d for x, d in zip(xs, t2)]
    return xs


def _gdn_kernel(*refs, ncc, ncx, hp, unroll):
    (qx, kx, vx, qc, kc, vc, rx0, rx1, rc0, rc1, dx0, dx1, dc0, dc1) = refs[:14]
    zx = refs[14:14 + hp]
    zc = refs[14 + hp:14 + 2 * hp]
    nw, ox, oc, ut_s, wq_s, at_s, kf_s, o_s = refs[14 + 2 * hp:]
    ri = lax.broadcasted_iota(jnp.int32, (GD_CHUNK, GD_CHUNK), 0)
    ci = lax.broadcasted_iota(jnp.int32, (GD_CHUNK, GD_CHUNK), 1)
    strict = (ci > ri, ci < ri)

    def local(units, base, q_ref, k_ref, v_ref, rows, dts):
        kts = [k_ref[p, i] for p, i in units]
        kbs = [kt.astype(BF16) for kt in kts]
        kks = [_dot_tn(kb, kb) for kb in kbs]
        qks = [_dot_tn(kb, q_ref[p, i].astype(BF16)) for kb, (p, i) in zip(kbs, units)]
        chains = [(u, d) for u in range(len(units)) for d in range(2)]
        mnegs = []
        for u, d in chains:
            p, i = units[u]
            mnegs.append(-(jnp.where(strict[d], dts[d][i, p], 0.0) * kks[u] * rows[d][i, p][0:1]))
        tts = _tri_inverse_multi(mnegs, ri, ci)
        sols = []
        for (u, d), tt in zip(chains, tts):
            p, i = units[u]
            rw = rows[d][i, p]
            rhs = jnp.concatenate([v_ref[p, i] * rw[0:1], kts[u] * rw[1:2]], axis=0).astype(BF16)
            sols.append(_dot(rhs, tt.astype(BF16)))
        for (u, d), sol in zip(chains, sols):
            p, i = units[u]
            n = base + i
            rw = rows[d][i, p]
            ut_s[p, d, n] = sol[:HEAD_DIM]
            wq_s[p, d, n, :, 0:GD_CHUNK] = sol[HEAD_DIM:].astype(BF16)
            wq_s[p, d, n, :, GD_CHUNK:2 * GD_CHUNK] = (q_ref[p, i] * rw[4:5]).astype(BF16)
            at_s[p, d, n] = (qks[u] * dts[d][i, p]).astype(BF16)
            kf_s[p, d, n] = (kts[u] * rw[2:3]).astype(BF16)
        for p, i in units:
            o_s[p, base + i] = jnp.zeros((HEAD_DIM, GD_CHUNK), F32)

    def seg_local(q_ref, k_ref, v_ref, rows, dts, count, base):
        per = min(unroll, count)

        def body(it, carry):
            units = [(p, it * per + u) for u in range(per) for p in range(hp)]
            local(units, base, q_ref, k_ref, v_ref, rows, dts)
            return carry
        lax.fori_loop(0, count // per, body, 0)

    seg_local(qc, kc, vc, (rc0, rc1), (dc0, dc1), ncc, 0)
    seg_local(qx, kx, vx, (rx0, rx1), (dx0, dx1), ncx, ncc)

    def seg_scan(states, rows, count, base):
        def body(i, carry):
            j = count - 1 - i
            chains = [(p, d, (i, j)[d]) for p in range(hp) for d in range(2)]
            ys = [_dot(st.astype(BF16), wq_s[p, d, base + c]) for st, (p, d, c) in zip(carry, chains)]
            vnbs = [(ut_s[p, d, base + c] - y[:, 0:GD_CHUNK]).astype(BF16) for y, (p, d, c) in zip(ys, chains)]
            ots = [_dot(vnb, at_s[p, d, base + c]) for vnb, (p, d, c) in zip(vnbs, chains)]
            upd = [_dot_nt(vnb, kf_s[p, d, base + c]) for vnb, (p, d, c) in zip(vnbs, chains)]
            for y, ot, (p, d, c) in zip(ys, ots, chains):
                o_s[p, base + c] += y[:, GD_CHUNK:2 * GD_CHUNK] + ot
            return tuple(st * rows[d][c, p][3:4] + du for st, du, (p, d, c) in zip(carry, upd, chains))
        return lax.fori_loop(0, count, body, states)

    zero = jnp.zeros((HEAD_DIM, HEAD_DIM), F32)
    states = seg_scan((zero,) * (2 * hp), (rc0, rc1), ncc, 0)
    seg_scan(states, (rx0, rx1), ncx, ncc)

    def seg_out(o_ref, z_refs, count, base):
        def body(i, carry):
            rs = pl.ds(pl.multiple_of(i * GD_CHUNK, GD_CHUNK), GD_CHUNK)
            for p in range(hp):
                ot = o_s[p, base + i]
                ms = jnp.mean(ot * ot, axis=0, keepdims=True)
                on = (ot * lax.rsqrt(ms + LN_EPS)).T
                z = z_refs[p][rs, :].astype(F32)
                o_ref[rs, p * HEAD_DIM:(p + 1) * HEAD_DIM] = (on * nw[...] * _silu(z)).astype(o_ref.dtype)
            return carry
        lax.fori_loop(0, count, body, 0)

    seg_out(oc, zc, ncc, 0)
    seg_out(ox, zx, ncx, ncc)


def _gdn_scan(gt_x, gt_c, rows_x, rows_c, dt_x, dt_c, px, pc, norm_w, l, b, t, lc, gd_heads, z_blk, hp, unroll):
    ncx, ncc = t // GD_CHUNK, lc // GD_CHUNK
    nc = ncx + ncc
    gd_d = gd_heads * HEAD_DIM
    nhb = gd_heads // hp

    def tile_spec(n, which):
        return pl.BlockSpec((hp, n, HEAD_DIM, GD_CHUNK), lambda i, h: (which * nhb + h, i, 0, 0))

    def row_spec(n, d):
        return pl.BlockSpec((n, hp, 8, GD_CHUNK), lambda i, h: (i, d * nhb + h, 0, 0))

    def dt_spec(n, d):
        return pl.BlockSpec((n, hp, GD_CHUNK, GD_CHUNK), lambda i, h: (i, d * nhb + h, 0, 0))

    def z_spec(n, p):
        return pl.BlockSpec((n, HEAD_DIM), lambda i, h: (i, z_blk + h * hp + p))

    return pl.pallas_call(
        functools.partial(_gdn_kernel, ncc=ncc, ncx=ncx, hp=hp, unroll=unroll),
        grid=(b, nhb),
        in_specs=[tile_spec(ncx, 0), tile_spec(ncx, 1), tile_spec(ncx, 2),
                  tile_spec(ncc, 0), tile_spec(ncc, 1), tile_spec(ncc, 2),
                  row_spec(ncx, 0), row_spec(ncx, 1), row_spec(ncc, 0), row_spec(ncc, 1),
                  dt_spec(ncx, 0), dt_spec(ncx, 1), dt_spec(ncc, 0), dt_spec(ncc, 1)]
                 + [z_spec(t, p) for p in range(hp)] + [z_spec(lc, p) for p in range(hp)]
                 + [pl.BlockSpec((None, 1, HEAD_DIM), lambda i, h: (l, 0, 0))],
        out_specs=[pl.BlockSpec((t, hp * HEAD_DIM), lambda i, h: (i, h)),
                   pl.BlockSpec((lc, hp * HEAD_DIM), lambda i, h: (i, h))],
        out_shape=[jax.ShapeDtypeStruct((b * t, gd_d), BF16),
                   jax.ShapeDtypeStruct((b * lc, gd_d), BF16)],
        scratch_shapes=[pltpu.VMEM((hp, 2, nc, HEAD_DIM, GD_CHUNK), F32),
                        pltpu.VMEM((hp, 2, nc, HEAD_DIM, 2 * GD_CHUNK), BF16),
                        pltpu.VMEM((hp, 2, nc, GD_CHUNK, GD_CHUNK), BF16),
                        pltpu.VMEM((hp, 2, nc, HEAD_DIM, GD_CHUNK), BF16),
                        pltpu.VMEM((hp, nc, HEAD_DIM, GD_CHUNK), F32)],
        compiler_params=_cparams(("parallel", "arbitrary")),
        name="gdn_scan",
    )(gt_x, gt_x, gt_x, gt_c, gt_c, gt_c, rows_x, rows_x, rows_c, rows_c, dt_x, dt_x, dt_c, dt_c,
      *([px] * hp), *([pc] * hp), norm_w)


def _outproj_kernel(x_ref, na_ref, cv_ref, gd_ref, w_ref, gt_ref, lg_ref, lb_ref, o_ref, *, alpha):
    n0 = na_ref.shape[1]
    n1 = n0 + cv_ref.shape[1]
    tm = x_ref.shape[0]
    nsub = 2 if tm % 32 == 0 else 1
    for k in range(nsub):
        rs = slice(k * tm // nsub, (k + 1) * tm // nsub)
        y = (_dot(na_ref[rs, :], w_ref[0:n0, :]) + _dot(cv_ref[rs, :], w_ref[n0:n1, :])
             + _dot(gd_ref[rs, :], w_ref[n1:, :]))
        z = alpha * x_ref[rs, :] + gt_ref[...] * y
        o_ref[rs, :] = _ln(z) * lg_ref[...] + lb_ref[...]


def _outproj(x, o_na, o_cv, o_gd, w_out, ada, ln_g, ln_b, l, grp, tm, alpha):
    m, d = x.shape
    d_mix = w_out.shape[1]
    return pl.pallas_call(
        functools.partial(_outproj_kernel, alpha=alpha),
        grid=(m // tm,),
        in_specs=[pl.BlockSpec((tm, d), lambda i: (i, 0)),
                  pl.BlockSpec((tm, o_na.shape[1]), lambda i: (i, 0)),
                  pl.BlockSpec((tm, o_cv.shape[1]), lambda i: (i, 0)),
                  pl.BlockSpec((tm, o_gd.shape[1]), lambda i: (i, 0)),
                  pl.BlockSpec((None, d_mix, d), lambda i: (l, 0, 0)),
                  _ada_spec(d, l, grp, 5), _ln_spec(d, l, 1), _ln_spec(d, l, 1)],
        out_specs=pl.BlockSpec((tm, d), lambda i: (i, 0)),
        out_shape=jax.ShapeDtypeStruct((m, d), F32),
        compiler_params=_cparams(("parallel",)),
        name="outproj",
    )(x, o_na, o_cv, o_gd, w_out, ada, ln_g, ln_b)


def _rope_tables(t):
    tok = jnp.arange(t, dtype=jnp.int32)
    row = (tok // GRID_W).astype(F32)
    col = (tok % GRID_W).astype(F32)
    n_freq = HEAD_DIM // 4
    inv_freq = ROPE_BASE ** (-jnp.arange(n_freq, dtype=F32) / n_freq)
    ar = (row[:, None] * inv_freq).T
    ac = (col[:, None] * inv_freq).T
    cos_t = jnp.concatenate([jnp.cos(ar), jnp.cos(ar), jnp.cos(ac), jnp.cos(ac)], axis=0)
    sin_t = jnp.concatenate([-jnp.sin(ar), jnp.sin(ar), -jnp.sin(ac), jnp.sin(ac)], axis=0)
    return cos_t, sin_t


def kernel(x, c, ctx, c_ctx, w_ada, b_ada, ln_g, ln_b, ffn1_w_gu, ffn1_w_down, w_in, na_rpb, cv_conv_w,
           gd_conv_w, gd_a_log, gd_dt_bias, gd_norm_w, w_out, ffn2_w_gu, ffn2_w_down):
    b, t, d = x.shape
    lc = ctx.shape[1]
    depth = w_ada.shape[0]
    d_ff = ffn1_w_down.shape[1]
    na_heads = na_rpb.shape[1]
    win_r, win_c = (na_rpb.shape[2] + 1) // 2, (na_rpb.shape[3] + 1) // 2
    cv_d = cv_conv_w.shape[2]
    cv_groups = cv_d // HEAD_DIM
    gd_heads = gd_a_log.shape[2]
    na_d, gd_d = na_heads * HEAD_DIM, gd_heads * HEAD_DIM
    ndh = 2 * gd_heads
    n_main = 3 * na_d + 3 * cv_d + 3 * gd_d + gd_d
    alpha = (2 * depth) ** 0.25
    assert b + 1 <= 8 and t % GD_CHUNK == 0 and lc % GD_CHUNK == 0 and ndh <= HEAD_DIM // 2
    assert t // GRID_W >= win_r

    tm = math.gcd(math.gcd(t, b * lc), 512)
    tm_big = math.gcd(math.gcd(t, b * lc), 1024)
    tf = 256 if d_ff % 256 == 0 else 128
    tf_b = 512 if d_ff % 512 == 0 else 128
    tn_ada = 1024 if (N_ADA * d) % 1024 == 0 else 128
    tn_in = math.gcd(n_main, 768)
    gd_hp = 2 if gd_heads % 2 == 0 else 1
    gd_unroll = 4
    na_rq = 4
    gd_cw = 3 if gd_heads % 3 == 0 else 1

    w_out_b = w_out.astype(BF16)
    half = HEAD_DIM // 2
    w_in_t = jnp.swapaxes(w_in, 1, 2)
    w_tail_t = jnp.zeros((depth, HEAD_DIM, d), F32)
    w_tail_t = w_tail_t.at[:, 0:ndh].set(w_in_t[:, n_main:n_main + ndh])
    w_tail_t = w_tail_t.at[:, half:half + ndh].set(w_in_t[:, n_main + ndh:n_main + 2 * ndh])
    alog_t = jnp.zeros((depth, HEAD_DIM, GD_CHUNK), F32).at[:, half:half + ndh, :].set(
        jnp.broadcast_to(gd_a_log.reshape(depth, ndh, 1), (depth, ndh, GD_CHUNK)))
    dtb_t = jnp.zeros((depth, HEAD_DIM, GD_CHUNK), F32).at[:, half:half + ndh, :].set(
        jnp.broadcast_to(gd_dt_bias.reshape(depth, ndh, 1), (depth, ndh, GD_CHUNK)))
    ln_g4 = ln_g.reshape(depth, 3, 1, d)
    ln_b4 = ln_b.reshape(depth, 3, 1, d)
    norm_w3 = gd_norm_w.reshape(depth, 1, HEAD_DIM)
    cos_x, sin_x = _rope_tables(t)
    cos_c, sin_c = jnp.ones((HEAD_DIM, lc), F32), jnp.zeros((HEAD_DIM, lc), F32)

    cvec = jnp.zeros((8, d), F32).at[0:b].set(c).at[b].set(c_ctx)
    ada = _ada_table(cvec, w_ada, b_ada, tn_ada).reshape(depth, 8, N_ADA, 1, d)

    grp_x = lambda i: (i * tm) // t
    grp_x_big = lambda i: (i * tm_big) // t
    grp_c = lambda i: b

    def ffn_pair(which, xv, cv, l, with_ctx):
        w_gu, w_dn = ((ffn1_w_gu, ffn1_w_down), (ffn2_w_gu, ffn2_w_down))[which]
        j0, k_ln = (0, 0) if which == 0 else (6, 2)
        f32_w = ("f32", w_gu, w_dn)
        if not with_ctx:
            return _ffn(xv, ada, ln_g4, ln_b4, l, j0, k_ln, grp_x_big, f32_w, tm_big, tf, alpha), cv
        cv, wg_b, wu_b, wd_b = _ffn(cv, ada, ln_g4, ln_b4, l, j0, k_ln, grp_c, f32_w, tm_big, tf, alpha, emit_tf=tf_b)
        xv = _ffn(xv, ada, ln_g4, ln_b4, l, j0, k_ln, grp_x, ("bf16", wg_b, wu_b, wd_b), tm, tf_b, alpha, ahead=True)
        return xv, cv

    xs = x.reshape(b * t, d)
    cs = ctx.reshape(b * lc, d)
    cv_blk = 3 * na_heads
    gd_blk = cv_blk + 3 * cv_groups
    z_blk = gd_blk + 3 * gd_heads
    na_rows = t // GRID_W
    assert na_rows % na_rq == 0 and na_rows >= win_r + na_rq and (na_rq * GRID_W) % HEAD_DIM == 0
    na_kinds, na_starts = _na_plan(na_rows, min(win_r, na_rows), win_r, na_rq)
    bias = _na_bias_table(na_rpb, na_kinds, na_rows, win_r, win_c, na_rq)

    for l in range(depth):
        last = l == depth - 1
        xs, cs = ffn_pair(0, xs, cs, l, True)

        p_x, tl_x = _inproj(xs, ada, l, grp_x_big, w_in_t, w_tail_t, n_main, tn_in, tm_big)
        p_c, tl_c = _inproj(cs, ada, l, grp_c, w_in_t, w_tail_t, n_main, tn_in, tm_big)

        o_na_x = _na_attention(p_x, p_c, bias, na_starts, l, b, t, lc, na_heads, win_r, na_rq)
        o_cv_x = _short_conv(p_x, cv_conv_w, l, b, t, cv_groups, cv_blk)

        rows_x, dt_x = _gdn_gates(tl_x, alog_t[l], dtb_t[l], gd_heads)
        rows_c, dt_c = _gdn_gates(tl_c, alog_t[l], dtb_t[l], gd_heads)
        gt_x = _gdn_prep(p_x, gd_conv_w, cos_x, sin_x, l, b, t, gd_blk, gd_heads, gd_cw)
        gt_c = _gdn_prep(p_c, gd_conv_w, cos_c, sin_c, l, b, lc, gd_blk, gd_heads, gd_heads)
        o_gd_x, o_gd_c = _gdn_scan(gt_x, gt_c, rows_x, rows_c, dt_x, dt_c, p_x, p_c, norm_w3, l,
                                   b, t, lc, gd_heads, z_blk, gd_hp, gd_unroll)

        xs = _outproj(xs, o_na_x, o_cv_x, o_gd_x, w_out_b, ada, ln_g4, ln_b4, l, grp_x, tm, alpha)
        if not last:
            o_na_c = _dense_attention(p_c, b, lc, na_heads)
            o_cv_c = _short_conv(p_c, cv_conv_w, l, b, lc, cv_groups, cv_blk)
            cs = _outproj(cs, o_na_c, o_cv_c, o_gd_c, w_out_b, ada, ln_g4, ln_b4, l, grp_c, tm, alpha)
        xs, cs = ffn_pair(1, xs, cs, l, not last)
    return xs.reshape(b, t, d)
```
